```python
import math
import jax, jax.numpy as jnp
from jax import lax
import numpy as np

D_MODEL = 4096
BATCH = 32
SEQ = 256
DEPTH = 2
DEC_BATCH = 2
DEC_SEQ = 4096
PAST_LEN = 256

GRID_W = 64
BLOCK_Q = 128
ROPE_THETA = 10000.0
LN_EPS = 1e-5
RMS_EPS = 1e-6
NEG_INF = -1e30

W_BRANCH = D_MODEL // 4
D_MIX = 4 * W_BRANCH

A_HEADS = 8
A_DK = W_BRANCH // (2 * A_HEADS)
A_DV = W_BRANCH // A_HEADS

B_HEADS = 8
B_KV = 2
B_DH = W_BRANCH // B_HEADS

D_HEADS = 8
D_KV = 2
D_DH = W_BRANCH // D_HEADS
WINDOW = 128

C_CH = 16
C_GROUPS = W_BRANCH // C_CH
C_STATE = 64

IN_SIZES = (W_BRANCH, W_BRANCH, W_BRANCH, W_BRANCH,
            W_BRANCH, B_KV * B_DH, B_KV * B_DH, W_BRANCH,
            W_BRANCH, W_BRANCH,
            W_BRANCH, D_KV * D_DH, D_KV * D_DH, W_BRANCH)
D_IN = sum(IN_SIZES)
DEEPNORM_ALPHA = (2 * DEPTH) ** 0.25
DEEPNORM_BETA = (8 * DEPTH) ** -0.25

kernel_name = "hybrid_diffusion_parallel_heads_step"


def _layernorm(x):
    xf = x.astype(jnp.float32)
    mu = jnp.mean(xf, -1, keepdims=True)
    var = jnp.mean(jnp.square(xf - mu), -1, keepdims=True)
    return ((xf - mu) * lax.rsqrt(var + LN_EPS)).astype(x.dtype)


def _rmsnorm(x, g):
    xf = x.astype(jnp.float32)
    return (xf * lax.rsqrt(jnp.mean(xf * xf, -1, keepdims=True) + RMS_EPS)).astype(x.dtype) * g


def _rope_tables(rows, cols, d):
    n_freq = d // 4
    inv = ROPE_THETA ** (-jnp.arange(n_freq, dtype=jnp.float32) / n_freq)
    ar = rows[:, None] * inv
    ac = cols[:, None] * inv
    ang = jnp.concatenate([ar, ar, ac, ac], axis=-1)
    return jnp.cos(ang), jnp.sin(ang)


def _apply_rope(x, cos, sin):
    r1, r2, c1, c2 = jnp.split(x, 4, axis=-1)
    rot = jnp.concatenate([-r2, r1, -c2, c1], axis=-1)
    return (x * cos[:, None, :] + rot * sin[:, None, :]).astype(x.dtype)


def _to_blocks(x):
    b, n = x.shape[:2]
    return jnp.moveaxis(x.reshape(b, n // BLOCK_Q, BLOCK_Q, *x.shape[2:]), 1, 0)


def _from_blocks(y):
    nb, b, bq = y.shape[:3]
    return jnp.moveaxis(y, 0, 1).reshape(b, nb * bq, *y.shape[3:])


def _sink_softmax(s, sink):
    sk = sink.astype(jnp.float32)[None, :, :, None, None]
    m = jnp.maximum(jnp.max(s, -1, keepdims=True), sk)
    p = jnp.exp(s - m)
    return p / (jnp.sum(p, -1, keepdims=True) + jnp.exp(sk - m))


def _diff_attention(q, k, v, lam):
    scale = A_DK ** -0.5

    def block(qb):
        s = jnp.einsum('bqhmd,bkhmd->bhmqk', qb, k).astype(jnp.float32) * scale
        a = jax.nn.softmax(s, axis=-1)
        w = a[:, :, 0] - lam * a[:, :, 1]
        return jnp.einsum('bhqk,bkhd->bqhd', w.astype(v.dtype), v)

    return _from_blocks(lax.map(block, _to_blocks(q)))


def _gqa_dense(q, k, v, sink=None):
    scale = q.shape[-1] ** -0.5

    def block(qb):
        s = jnp.einsum('bqhgd,bkhd->bhgqk', qb, k).astype(jnp.float32) * scale
        a = jax.nn.softmax(s, axis=-1) if sink is None else _sink_softmax(s, sink)
        return jnp.einsum('bhgqk,bkhd->bqhgd', a.astype(v.dtype), v)

    return _from_blocks(lax.map(block, _to_blocks(q)))


def _gqa_window(q, k_lat, v_lat, k_ctx, v_ctx, sink):
    n = q.shape[1]
    n_ctx = k_ctx.shape[1]
    span = BLOCK_Q + 2 * WINDOW
    pad = ((0, 0), (WINDOW, WINDOW), (0, 0), (0, 0))
    kp = jnp.pad(k_lat, pad)
    vp = jnp.pad(v_lat, pad)
    scale = q.shape[-1] ** -0.5
    ctx_mask = jnp.ones((BLOCK_Q, n_ctx), dtype=bool)

    def block(args):
        i, qb = args
        start = i * BLOCK_Q
        kb = jnp.concatenate([lax.dynamic_slice_in_dim(kp, start, span, axis=1), k_ctx], axis=1)
        vb = jnp.concatenate([lax.dynamic_slice_in_dim(vp, start, span, axis=1), v_ctx], axis=1)
        q_pos = start + jnp.arange(BLOCK_Q)
        k_pos = start - WINDOW + jnp.arange(span)
        band = (k_pos[None, :] >= 0) & (k_pos[None, :] < n) & (jnp.abs(q_pos[:, None] - k_pos[None, :]) <= WINDOW)
        mask = jnp.concatenate([band, ctx_mask], axis=-1)
        s = jnp.einsum('bqhgd,bkhd->bhgqk', qb, kb).astype(jnp.float32) * scale
        s = jnp.where(mask, s, NEG_INF)
        a = _sink_softmax(s, sink)
        return jnp.einsum('bhgqk,bkhd->bqhgd', a.astype(vb.dtype), vb)

    nb = n // BLOCK_Q
    return _from_blocks(lax.map(block, (jnp.arange(nb), _to_blocks(q))))


def _s5_discretize(lam_re, lam_im, log_dt, b_re, b_im):
    f32 = jnp.float32
    lam_re, lam_im, b_re, b_im = (t.astype(f32) for t in (lam_re, lam_im, b_re, b_im))
    dt = jnp.exp(log_dt.astype(f32))[:, None]
    mag = jnp.exp(lam_re * dt)
    a_re = mag * jnp.cos(lam_im * dt)
    a_im = mag * jnp.sin(lam_im * dt)
    den = lam_re * lam_re + lam_im * lam_im
    nr = a_re - 1.0
    f_re = (nr * lam_re + a_im * lam_im) / den
    f_im = (a_im * lam_re - nr * lam_im) / den
    bb_re = f_re[..., None] * b_re - f_im[..., None] * b_im
    bb_im = f_re[..., None] * b_im + f_im[..., None] * b_re
    return a_re, a_im, bb_re, bb_im


def _complex_combine(e1, e2):
    a1r, a1i, b1r, b1i = e1
    a2r, a2i, b2r, b2i = e2
    return (a2r * a1r - a2i * a1i, a2r * a1i + a2i * a1r,
            a2r * b1r - a2i * b1i + b2r, a2r * b1i + a2i * b1r + b2i)


def _s5_branch(u, gate, lp, h0=None):
    f32 = jnp.float32
    b, n = u.shape[:2]
    uf = u.astype(f32).reshape(b, n, C_GROUPS, C_CH)
    y = uf * lp['c_d'].astype(f32).reshape(C_GROUPS, C_CH)
    fin_re, fin_im = [], []
    for d in range(2):
        a_re, a_im, bb_re, bb_im = _s5_discretize(lp['c_lam_re'][d], lp['c_lam_im'][d], lp['c_log_dt'][d],
                                                  lp['c_b_re'][d], lp['c_b_im'][d])
        ud = uf if d == 0 else jnp.flip(uf, axis=1)
        bu_re = jnp.einsum('gpc,blgc->blgp', bb_re, ud)
        bu_im = jnp.einsum('gpc,blgc->blgp', bb_im, ud)
        if h0 is not None:
            h_re = h0[0][:, d].astype(f32)
            h_im = h0[1][:, d].astype(f32)
            bu_re = bu_re.at[:, 0].add(a_re * h_re - a_im * h_im)
            bu_im = bu_im.at[:, 0].add(a_re * h_im + a_im * h_re)
        elems = (jnp.broadcast_to(a_re, bu_re.shape), jnp.broadcast_to(a_im, bu_im.shape), bu_re, bu_im)
        _, _, hr, hi = lax.associative_scan(_complex_combine, elems, axis=1)
        if h0 is None:
            fin_re.append(hr[:, -1])
            fin_im.append(hi[:, -1])
        if d == 1:
            hr = jnp.flip(hr, axis=1)
            hi = jnp.flip(hi, axis=1)
        c_re = lp['c_c_re'][d].astype(f32)
        c_im = lp['c_c_im'][d].astype(f32)
        y = y + jnp.einsum('gcp,blgp->blgc', c_re, hr) - jnp.einsum('gcp,blgp->blgc', c_im, hi)
    y = jax.nn.gelu(y.reshape(b, n, W_BRANCH))
    y = y * jax.nn.sigmoid(y @ lp['c_glu_w'].astype(f32) + lp['c_glu_b'].astype(f32))
    out = y.astype(u.dtype) * jax.nn.silu(gate)
    finals = (jnp.stack(fin_re, axis=1), jnp.stack(fin_im, axis=1)) if h0 is None else None
    return out, finals


def _mixer_inputs(x, cond, lp):
    ada = jax.nn.silu(cond) @ lp['w_ada'] + lp['b_ada']
    shift, scale, gate = jnp.split(ada[:, None, :], 3, axis=-1)
    h = _layernorm(x) * (1.0 + scale) + shift
    idx = np.cumsum(IN_SIZES)[:-1].tolist()
    return jnp.split(h @ lp['w_in'], idx, axis=-1), gate


def _residual(x, outs, gate, lp):
    y = jnp.concatenate(outs, axis=-1) @ lp['w_out']
    return _layernorm(DEEPNORM_ALPHA * x + gate * y) * lp['ln_g'] + lp['ln_b']


def _diff_lambda(lp, lam_init):
    f = lambda t: t.astype(jnp.float32)
    return (jnp.exp(jnp.sum(f(lp['a_lam_q1']) * f(lp['a_lam_k1'])))
            - jnp.exp(jnp.sum(f(lp['a_lam_q2']) * f(lp['a_lam_k2']))) + lam_init)


def _diff_output(o, lp, lam_init, gate):
    b, n = o.shape[:2]
    return (_rmsnorm(o, lp['a_subln_g']) * (1.0 - lam_init)).reshape(b, n, W_BRANCH) * jax.nn.silu(gate)


def _context_layer(x, c_ctx, lp, lam_init):
    b, n = x.shape[:2]
    parts, gate = _mixer_inputs(x, c_ctx[None], lp)
    a_q, a_k, a_v, a_g, b_q, b_k, b_v, b_g, c_u, c_g, d_q, d_k, d_v, d_g = parts
    ak = a_k.reshape(b, n, A_HEADS, 2, A_DK)
    av = a_v.reshape(b, n, A_HEADS, A_DV)
    lam = _diff_lambda(lp, lam_init)
    out_a = _diff_output(_diff_attention(a_q.reshape(b, n, A_HEADS, 2, A_DK), ak, av, lam), lp, lam_init, a_g)
    bq = _rmsnorm(b_q.reshape(b, n, B_HEADS, B_DH), lp['b_qnorm_g']).reshape(b, n, B_KV, B_HEADS // B_KV, B_DH)
    bk = _rmsnorm(b_k.reshape(b, n, B_KV, B_DH), lp['b_knorm_g'])
    bv = b_v.reshape(b, n, B_KV, B_DH)
    out_b = _gqa_dense(bq, bk, bv).reshape(b, n, W_BRANCH) * jax.nn.silu(b_g)
    out_c, (s_re, s_im) = _s5_branch(c_u, c_g, lp)
    dq = d_q.reshape(b, n, D_KV, D_HEADS // D_KV, D_DH)
    dk = d_k.reshape(b, n, D_KV, D_DH)
    dv = d_v.reshape(b, n, D_KV, D_DH)
    out_d = _gqa_dense(dq, dk, dv, sink=lp['d_sink'].reshape(D_KV, D_HEADS // D_KV)).reshape(b, n, W_BRANCH) * jax.nn.silu(d_g)
    y = _residual(x, (out_a, out_b, out_c, out_d), gate, lp)
    return y, (ak.reshape(b, n, A_HEADS, 2 * A_DK), av, bk, bv, dk, dv, s_re, s_im)


def _latent_layer(x, c, ctx, lp, lam_init, rope_a, rope_h):
    b, n = x.shape[:2]
    ak_c, av_c, bk_c, bv_c, dk_c, dv_c, s_re_c, s_im_c = ctx
    m = ak_c.shape[1]
    parts, gate = _mixer_inputs(x, c, lp)
    a_q, a_k, a_v, a_g, b_q, b_k, b_v, b_g, c_u, c_g, d_q, d_k, d_v, d_g = parts
    aq = _apply_rope(a_q.reshape(b, n, 2 * A_HEADS, A_DK), *rope_a).reshape(b, n, A_HEADS, 2, A_DK)
    ak = _apply_rope(a_k.reshape(b, n, 2 * A_HEADS, A_DK), *rope_a).reshape(b, n, A_HEADS, 2, A_DK)
    ak_all = jnp.concatenate([ak, ak_c.reshape(b, m, A_HEADS, 2, A_DK)], axis=1)
    av_all = jnp.concatenate([a_v.reshape(b, n, A_HEADS, A_DV), av_c], axis=1)
    lam = _diff_lambda(lp, lam_init)
    out_a = _diff_output(_diff_attention(aq, ak_all, av_all, lam), lp, lam_init, a_g)
    bq = _apply_rope(_rmsnorm(b_q.reshape(b, n, B_HEADS, B_DH), lp['b_qnorm_g']), *rope_h)
    bq = bq.reshape(b, n, B_KV, B_HEADS // B_KV, B_DH)
    bk = _apply_rope(_rmsnorm(b_k.reshape(b, n, B_KV, B_DH), lp['b_knorm_g']), *rope_h)
    bk_all = jnp.concatenate([bk, bk_c], axis=1)
    bv_all = jnp.concatenate([b_v.reshape(b, n, B_KV, B_DH), bv_c], axis=1)
    out_b = _gqa_dense(bq, bk_all, bv_all).reshape(b, n, W_BRANCH) * jax.nn.silu(b_g)
    out_c, _ = _s5_branch(c_u, c_g, lp, h0=(s_re_c, s_im_c))
    dq = _apply_rope(d_q.reshape(b, n, D_HEADS, D_DH), *rope_h).reshape(b, n, D_KV, D_HEADS // D_KV, D_DH)
    dk = _apply_rope(d_k.reshape(b, n, D_KV, D_DH), *rope_h)
    dv = d_v.reshape(b, n, D_KV, D_DH)
    out_d = _gqa_window(dq, dk, dv, dk_c, dv_c, lp['d_sink'].reshape(D_KV, D_HEADS // D_KV))
    out_d = out_d.reshape(b, n, W_BRANCH) * jax.nn.silu(d_g)
    return _residual(x, (out_a, out_b, out_c, out_d), gate, lp)


def setup_inputs(seed: int = 0) -> dict:
    key = jax.random.key(seed)
    ks = iter(jax.random.split(key, 48))
    f32 = jnp.float32

    def nrm(shape, s=1.0):
        return s * jax.random.normal(next(ks), shape, f32)

    inp = {}
    inp['x_prompt'] = nrm((BATCH, SEQ, D_MODEL))
    inp['x_sample'] = nrm((DEC_BATCH, DEC_SEQ, D_MODEL))
    inp['c'] = nrm((DEC_BATCH, D_MODEL))
    inp['cache_a_k'] = nrm((DEC_BATCH, DEPTH, PAST_LEN, A_HEADS, 2 * A_DK))
    inp['cache_a_v'] = nrm((DEC_BATCH, DEPTH, PAST_LEN, A_HEADS, A_DV))
    inp['cache_b_k'] = nrm((DEC_BATCH, DEPTH, PAST_LEN, B_KV, B_DH))
    inp['cache_b_v'] = nrm((DEC_BATCH, DEPTH, PAST_LEN, B_KV, B_DH))
    inp['cache_d_k'] = nrm((DEC_BATCH, DEPTH, PAST_LEN, D_KV, D_DH))
    inp['cache_d_v'] = nrm((DEC_BATCH, DEPTH, PAST_LEN, D_KV, D_DH))
    inp['state_c_re'] = nrm((DEC_BATCH, DEPTH, 2, C_GROUPS, C_STATE), 0.1)
    inp['state_c_im'] = nrm((DEC_BATCH, DEPTH, 2, C_GROUPS, C_STATE), 0.1)
    inp['c_ctx'] = nrm((D_MODEL,))
    inp['w_ada'] = nrm((DEPTH, D_MODEL, 3 * D_MODEL), 0.5 * D_MODEL ** -0.5)
    inp['b_ada'] = nrm((DEPTH, 3 * D_MODEL), 0.01)
    inp['w_in'] = nrm((DEPTH, D_MODEL, D_IN), D_MODEL ** -0.5)
    inp['w_out'] = nrm((DEPTH, D_MIX, D_MODEL), DEEPNORM_BETA * D_MIX ** -0.5)
    inp['ln_g'] = 1.0 + nrm((DEPTH, D_MODEL), 0.01)
    inp['ln_b'] = nrm((DEPTH, D_MODEL), 0.01)
    inp['a_lam_q1'] = nrm((DEPTH, A_DK), 0.1)
    inp['a_lam_k1'] = nrm((DEPTH, A_DK), 0.1)
    inp['a_lam_q2'] = nrm((DEPTH, A_DK), 0.1)
    inp['a_lam_k2'] = nrm((DEPTH, A_DK), 0.1)
    inp['a_subln_g'] = 1.0 + nrm((DEPTH, A_DV), 0.01)
    inp['b_qnorm_g'] = 1.0 + nrm((DEPTH, B_DH), 0.01)
    inp['b_knorm_g'] = 1.0 + nrm((DEPTH, B_DH), 0.01)
    inp['d_sink'] = nrm((DEPTH, D_HEADS), 0.5)
    inp['c_lam_re'] = -0.5 + nrm((DEPTH, 2, C_GROUPS, C_STATE), 0.01)
    inp['c_lam_im'] = jnp.pi * jnp.arange(C_STATE, dtype=f32) + nrm((DEPTH, 2, C_GROUPS, C_STATE), 0.01)
    inp['c_log_dt'] = jax.random.uniform(next(ks), (DEPTH, 2, C_GROUPS), f32,
                                         minval=math.log(1e-3), maxval=math.log(1e-1))
    inp['c_b_re'] = nrm((DEPTH, 2, C_GROUPS, C_STATE, C_CH), (2 * C_CH) ** -0.5)
    inp['c_b_im'] = nrm((DEPTH, 2, C_GROUPS, C_STATE, C_CH), (2 * C_CH) ** -0.5)
    inp['c_c_re'] = nrm((DEPTH, 2, C_GROUPS, C_CH, C_STATE), (2 * C_STATE) ** -0.5)
    inp['c_c_im'] = nrm((DEPTH, 2, C_GROUPS, C_CH, C_STATE), (2 * C_STATE) ** -0.5)
    inp['c_d'] = nrm((DEPTH, W_BRANCH))
    inp['c_glu_w'] = nrm((DEPTH, W_BRANCH, W_BRANCH), W_BRANCH ** -0.5)
    inp['c_glu_b'] = nrm((DEPTH, W_BRANCH), 0.01)
    return inp


def reference(x_prompt, x_sample, c, cache_a_k, cache_a_v, cache_b_k, cache_b_v, cache_d_k, cache_d_v,
              state_c_re, state_c_im, c_ctx, w_ada, b_ada, w_in, w_out, ln_g, ln_b,
              a_lam_q1, a_lam_k1, a_lam_q2, a_lam_k2, a_subln_g, b_qnorm_g, b_knorm_g, d_sink,
              c_lam_re, c_lam_im, c_log_dt, c_b_re, c_b_im, c_c_re, c_c_im, c_d, c_glu_w, c_glu_b):
    layers = []
    for l in range(DEPTH):
        layers.append(dict(w_ada=w_ada[l], b_ada=b_ada[l], w_in=w_in[l], w_out=w_out[l], ln_g=ln_g[l], ln_b=ln_b[l],
                           a_lam_q1=a_lam_q1[l], a_lam_k1=a_lam_k1[l], a_lam_q2=a_lam_q2[l], a_lam_k2=a_lam_k2[l],
                           a_subln_g=a_subln_g[l], b_qnorm_g=b_qnorm_g[l], b_knorm_g=b_knorm_g[l], d_sink=d_sink[l],
                           c_lam_re=c_lam_re[l], c_lam_im=c_lam_im[l], c_log_dt=c_log_dt[l],
                           c_b_re=c_b_re[l], c_b_im=c_b_im[l], c_c_re=c_c_re[l], c_c_im=c_c_im[l],
                           c_d=c_d[l], c_glu_w=c_glu_w[l], c_glu_b=c_glu_b[l]))
    lam_inits = [0.8 - 0.6 * math.exp(-0.3 * l) for l in range(DEPTH)]

    y_prompt = x_prompt
    ctx_list = []
    for l in range(DEPTH):
        y_prompt, ctx_t = _context_layer(y_prompt, c_ctx, layers[l], lam_inits[l])
        ctx_list.append(ctx_t)
    ns = [jnp.stack(z, axis=1) for z in zip(*ctx_list)]

    n_lat = x_sample.shape[1]
    n_rows = n_lat // GRID_W
    rows = jnp.repeat(jnp.arange(n_rows, dtype=jnp.float32), GRID_W)
    cols = jnp.tile(jnp.arange(GRID_W, dtype=jnp.float32), n_rows)
    rope_a = _rope_tables(rows, cols, A_DK)
    rope_h = _rope_tables(rows, cols, B_DH)
    y_sample = x_sample
    for l in range(DEPTH):
        ctx_l = (cache_a_k[:, l], cache_a_v[:, l], cache_b_k[:, l], cache_b_v[:, l],
                 cache_d_k[:, l], cache_d_v[:, l], state_c_re[:, l], state_c_im[:, l])
        y_sample = _latent_layer(y_sample, c, ctx_l, layers[l], lam_inits[l], rope_a, rope_h)

    return (y_prompt, y_sample, ns[0], ns[1], ns[2], ns[3], ns[4], ns[5], ns[6], ns[7])
```

```python
import functools
import math

import jax
import jax.numpy as jnp
import numpy as np
from jax import lax
from jax.experimental import pallas as pl
from jax.experimental.pallas import tpu as pltpu

F32 = jnp.float32
BF16 = jnp.bfloat16

D_MODEL = 4096
DEPTH = 2
GRID_W = 64
ROPE_THETA = 10000.0
LN_EPS = 1e-5
RMS_EPS = 1e-6
NEG_INF = -1e30
WINDOW = 128

W_BRANCH = D_MODEL // 4
HEADS = 8
KV_HEADS = 2
GQA = HEADS // KV_HEADS
DH = W_BRANCH // HEADS
A_DK = DH // 2
C_CH = 16
C_GROUPS = W_BRANCH // C_CH
C_STATE = 64
GROUPS_PER_BLOCK = 8
N_GBLK = C_GROUPS // GROUPS_PER_BLOCK
STATE_LANES = GROUPS_PER_BLOCK * C_STATE
CHAINS = 8

IN_SIZES = (W_BRANCH, W_BRANCH, W_BRANCH, W_BRANCH,
            W_BRANCH, KV_HEADS * DH, KV_HEADS * DH, W_BRANCH,
            W_BRANCH, W_BRANCH,
            W_BRANCH, KV_HEADS * DH, KV_HEADS * DH, W_BRANCH)
D_IN = sum(IN_SIZES)
_SEG_ORDER = (0, 1, 2, 3, 4, 7, 8, 9, 10, 13, 5, 6, 11, 12)
COL_AQ, COL_AK, COL_AV, COL_AG = 0, 1024, 2048, 3072
COL_BQ, COL_BG = 4096, 5120
COL_CU, COL_CG = 6144, 7168
COL_DQ, COL_DG = 8192, 9216
COL_KV = 10240
DEEPNORM_ALPHA = (2 * DEPTH) ** 0.25

A_SCALE = A_DK ** -0.5
H_SCALE = DH ** -0.5

VMEM_LIMIT = 56 * 1024 * 1024


def _cparams(n_axes):
    return pltpu.CompilerParams(dimension_semantics=("arbitrary",) * n_axes,
                                vmem_limit_bytes=VMEM_LIMIT)


def _dot(a, b):
    return jnp.dot(a, b, preferred_element_type=F32)


def _dot_nt(a, b):
    return lax.dot_general(a, b, (((1,), (1,)), ((), ())), preferred_element_type=F32)


def _silu(x):
    return x * jax.nn.sigmoid(x)


def _rms(x):
    return x * lax.rsqrt(jnp.mean(x * x, axis=-1, keepdims=True) + RMS_EPS)


def _rope(x, cos, sin_signed, quarter):
    lane = lax.broadcasted_iota(jnp.int32, (1, x.shape[-1]), 1)
    first = (lane % (2 * quarter)) < quarter
    rot = jnp.where(first, pltpu.roll(x, x.shape[-1] - quarter, 1), pltpu.roll(x, quarter, 1))
    return x * cos + rot * sin_signed


N_COND = 3
ADA_TN = 512


def _ada_kernel(c_ref, w_ref, b_ref, o_ref, sc_ref):
    @pl.when((pl.program_id(0) == 0) & (pl.program_id(1) == 0))
    def _():
        sc_ref[...] = _silu(c_ref[...])

    tn = w_ref.shape[1]

    def body(i, accs):
        r0 = pl.multiple_of(i * 8, 8)
        w = w_ref[pl.ds(r0, 8), :]
        cs = sc_ref[pl.ds(r0, 8), :]
        return tuple(acc + w * cs[:, r:r + 1] for r, acc in enumerate(accs))

    accs = lax.fori_loop(0, w_ref.shape[0] // 8, body,
                         tuple(jnp.zeros((8, tn), F32) for _ in range(N_COND)), unroll=4)
    row_id = lax.broadcasted_iota(jnp.int32, (8, tn), 0)
    out = jnp.zeros((8, tn), F32)
    for r, a in enumerate(accs):
        out = jnp.where(row_id == r, jnp.sum(a, axis=0, keepdims=True), out)
    o_ref[...] = out + b_ref[...]


def _ada(cond_cols, w_ada, b_ada):
    depth, d, n = w_ada.shape
    return pl.pallas_call(
        _ada_kernel,
        grid=(depth, n // ADA_TN),
        in_specs=[pl.BlockSpec((d, 8), lambda l, j: (0, 0)),
                  pl.BlockSpec((None, d, ADA_TN), lambda l, j: (l, 0, j)),
                  pl.BlockSpec((None, 1, ADA_TN), lambda l, j: (l, 0, j))],
        out_specs=pl.BlockSpec((None, 8, ADA_TN), lambda l, j: (l, 0, j)),
        out_shape=jax.ShapeDtypeStruct((depth, 8, n), F32),
        scratch_shapes=[pltpu.VMEM((d, 8), F32)],
        compiler_params=_cparams(2),
        name="ada",
    )(cond_cols, w_ada, b_ada.reshape(depth, 1, n))


WIN_TM = 512
WIN_TN = 1024
LN_ROWS = 32


def _ln_win_kernel(x_ref, sh_ref, sc_ref, w_ref, o_ref, h_ref):
    @pl.when(pl.program_id(1) == 0)
    def _():
        shift = sh_ref[...]
        scale1 = 1.0 + sc_ref[...]

        def body(i, carry):
            r0 = pl.multiple_of(i * LN_ROWS, LN_ROWS)
            x = x_ref[pl.ds(r0, LN_ROWS), :]
            mu = jnp.mean(x, axis=-1, keepdims=True)
            xc = x - mu
            var = jnp.mean(xc * xc, axis=-1, keepdims=True)
            h = xc * lax.rsqrt(var + LN_EPS) * scale1 + shift
            h_ref[pl.ds(r0, LN_ROWS), :] = h.astype(BF16)
            return carry

        lax.fori_loop(0, x_ref.shape[0] // LN_ROWS, body, 0)

    o_ref[...] = _dot(h_ref[...], w_ref[...]).astype(o_ref.dtype)


def _ln_win(x, ada_l, w_in_bf16, row_of_tile):
    m, d = x.shape
    n = w_in_bf16.shape[1]
    return pl.pallas_call(
        _ln_win_kernel,
        grid=(m // WIN_TM, n // WIN_TN),
        in_specs=[pl.BlockSpec((WIN_TM, d), lambda i, j: (i, 0)),
                  pl.BlockSpec((None, 1, d), lambda i, j: (row_of_tile(i), 0, 0)),
                  pl.BlockSpec((None, 1, d), lambda i, j: (row_of_tile(i), 0, 1)),
                  pl.BlockSpec((d, WIN_TN), lambda i, j: (0, j))],
        out_specs=pl.BlockSpec((WIN_TM, WIN_TN), lambda i, j: (i, j)),
        out_shape=jax.ShapeDtypeStruct((m, n), F32),
        scratch_shapes=[pltpu.VMEM((WIN_TM, d), BF16)],
        compiler_params=_cparams(2),
        name="ln_win",
    )(x, ada_l, ada_l, w_in_bf16)


def _diff_lambda(l4, lam_init):
    e1 = jnp.exp(jnp.sum(l4[0:1] * l4[1:2], axis=-1, keepdims=True))
    e2 = jnp.exp(jnp.sum(l4[2:3] * l4[3:4], axis=-1, keepdims=True))
    return e1 - e2 + lam_init


def _split_maps(q):
    lane = lax.broadcasted_iota(jnp.int32, (1, DH), 1)
    first = lane < A_DK
    return jnp.where(first, q, 0.0).astype(BF16), jnp.where(first, 0.0, q).astype(BF16)


def _softmax_pv(s_list, v_list, sink=None):
    m = s_list[0].max(axis=-1, keepdims=True)
    for s in s_list[1:]:
        m = jnp.maximum(m, s.max(axis=-1, keepdims=True))
    if sink is not None:
        m = jnp.maximum(m, sink)
    den = None
    o = None
    for s, v in zip(s_list, v_list):
        p = jnp.exp(s - m)
        ps = p.sum(axis=-1, keepdims=True)
        den = ps if den is None else den + ps
        pv = _dot(p.astype(BF16), v)
        o = pv if o is None else o + pv
    if sink is not None:
        den = den + jnp.exp(sink - m)
    return o / den


def _diff_finish(o0, o1, lam, lam_init, subg, gate):
    o = o0 - lam * o1
    return _rms(o) * subg * (1.0 - lam_init) * _silu(gate)


def _ctx_attn_kernel(aq, ak, av, ag, bq, bg, dq, dg, kv, l4, subg, qng, kng, sink,
                     oa, ob, od, obk, *, lam_init):
    lam = _diff_lambda(l4[...], lam_init)
    for h in range(HEADS):
        sl = slice(h * DH, (h + 1) * DH)
        q0, q1 = _split_maps(aq[:, sl] * A_SCALE)
        k = ak[:, sl].astype(BF16)
        v = av[:, sl].astype(BF16)
        o0 = _softmax_pv([_dot_nt(q0, k)], [v])
        o1 = _softmax_pv([_dot_nt(q1, k)], [v])
        oa[:, sl] = _diff_finish(o0, o1, lam, lam_init, subg[...], ag[:, sl]).astype(oa.dtype)

    kvw = KV_HEADS * DH
    for kvh in range(KV_HEADS):
        ksl = slice(kvh * DH, (kvh + 1) * DH)
        bk = _rms(kv[:, ksl]) * kng[...]
        obk[:, ksl] = bk
        bkb = bk.astype(BF16)
        bvb = kv[:, kvw + kvh * DH: kvw + (kvh + 1) * DH].astype(BF16)
        dkb = kv[:, 2 * kvw + kvh * DH: 2 * kvw + (kvh + 1) * DH].astype(BF16)
        dvb = kv[:, 3 * kvw + kvh * DH: 3 * kvw + (kvh + 1) * DH].astype(BF16)
        for g in range(GQA):
            h = kvh * GQA + g
            sl = slice(h * DH, (h + 1) * DH)
            q = (_rms(bq[:, sl]) * qng[...] * H_SCALE).astype(BF16)
            o = _softmax_pv([_dot_nt(q, bkb)], [bvb])
            ob[:, sl] = (o * _silu(bg[:, sl])).astype(ob.dtype)
            q = (dq[:, sl] * H_SCALE).astype(BF16)
            o = _softmax_pv([_dot_nt(q, dkb)], [dvb], sink=sink[:, h:h + 1])
            od[:, sl] = (o * _silu(dg[:, sl])).astype(od.dtype)


def _ctx_attn(proj, seq, l4, subg, qng, kng, sink, lam_init):
    m = proj.shape[0]
    wb = W_BRANCH

    def col(c):
        return pl.BlockSpec((seq, wb), lambda b, c=c: (b, c // wb))

    def full(a):
        return pl.BlockSpec(a.shape, lambda b: (0,) * a.ndim)

    outs = pl.pallas_call(
        functools.partial(_ctx_attn_kernel, lam_init=lam_init),
        grid=(m // seq,),
        in_specs=[col(COL_AQ), col(COL_AK), col(COL_AV), col(COL_AG), col(COL_BQ), col(COL_BG),
                  col(COL_DQ), col(COL_DG), col(COL_KV),
                  full(l4), full(subg), full(qng), full(kng), full(sink)],
        out_specs=[pl.BlockSpec((seq, wb), lambda b: (b, 0))] * 3
        + [pl.BlockSpec((seq, KV_HEADS * DH), lambda b: (b, 0))],
        out_shape=[jax.ShapeDtypeStruct((m, wb), BF16)] * 3
        + [jax.ShapeDtypeStruct((m, KV_HEADS * DH), F32)],
        compiler_params=_cparams(1),
        name="ctx_attn",
    )(*([proj] * 9), l4, subg, qng, kng, sink)
    return outs


PREP_TR = 512


def _kv_prep_kernel(ak, av, kv, cosa, sina, cosh, sinh, kng, oak, oav, okv):
    for h in range(HEADS):
        sl = slice(h * DH, (h + 1) * DH)
        oak[:, sl] = _rope(ak[:, sl], cosa[...], sina[...], A_DK // 4).astype(BF16)
    oav[...] = av[...].astype(BF16)
    kvw = KV_HEADS * DH
    for kvh in range(KV_HEADS):
        ksl = slice(kvh * DH, (kvh + 1) * DH)
        bk = _rms(kv[:, ksl]) * kng[...]
        okv[:, ksl] = _rope(bk, cosh[...], sinh[...], DH // 4).astype(BF16)
        dsl = slice(2 * kvw + kvh * DH, 2 * kvw + (kvh + 1) * DH)
        okv[:, dsl] = _rope(kv[:, dsl], cosh[...], sinh[...], DH // 4).astype(BF16)
    okv[:, kvw:2 * kvw] = kv[:, kvw:2 * kvw].astype(BF16)
    okv[:, 3 * kvw:4 * kvw] = kv[:, 3 * kvw:4 * kvw].astype(BF16)


def _kv_prep(proj, n_lat, rope_a, rope_h, kng):
    m = proj.shape[0]
    wb = W_BRANCH
    nt = n_lat // PREP_TR

    def col(c):
        return pl.BlockSpec((PREP_TR, wb), lambda i, c=c: (i, c // wb))

    tab = pl.BlockSpec((PREP_TR, DH), lambda i: (i % nt, 0))
    return pl.pallas_call(
        _kv_prep_kernel,
        grid=(m // PREP_TR,),
        in_specs=[col(COL_AK), col(COL_AV), col(COL_KV), tab, tab, tab, tab,
                  pl.BlockSpec(kng.shape, lambda i: (0, 0))],
        out_specs=[pl.BlockSpec((PREP_TR, wb), lambda i: (i, 0))] * 3,
        out_shape=[jax.ShapeDtypeStruct((m, wb), BF16)] * 3,
        compiler_params=_cparams(1),
        name="kv_prep",
    )(proj, proj, proj, rope_a[0], rope_a[1], rope_h[0], rope_h[1], kng)


LAT_TQ = 256


def _lat_a_kernel(q_ref, g_ref, cos, sin, k_ref, v_ref, kc_ref, vc_ref, l4, subg, o_ref, kall, vall,
                  *, lam_init, n_lat):
    @pl.when(pl.program_id(2) == 0)
    def _():
        kall[0:n_lat, :] = k_ref[...]
        kall[n_lat:, :] = kc_ref[...].astype(BF16)
        vall[0:n_lat, :] = v_ref[...]
        vall[n_lat:, :] = vc_ref[...].astype(BF16)

    lam = _diff_lambda(l4[...], lam_init)
    q = _rope(q_ref[...], cos[...], sin[...], A_DK // 4) * A_SCALE
    q0, q1 = _split_maps(q)
    k = kall[...]
    v = vall[...]
    o0 = _softmax_pv([_dot_nt(q0, k)], [v])
    o1 = _softmax_pv([_dot_nt(q1, k)], [v])
    o_ref[...] = _diff_finish(o0, o1, lam, lam_init, subg[...], g_ref[...]).astype(o_ref.dtype)


def _lat_a(proj, akr, avb, cache_k, cache_v, layer, rope_a, l4, subg, lam_init, n_b, n_lat):
    nq = n_lat // LAT_TQ
    n_ctx = cache_k.shape[2]

    def qcol(c):
        return pl.BlockSpec((LAT_TQ, DH), lambda b, h, i, c=c: (b * nq + i, c // DH + h))

    tab = pl.BlockSpec((LAT_TQ, DH), lambda b, h, i: (i, 0))
    kvs = pl.BlockSpec((n_lat, DH), lambda b, h, i: (b, h))
    cs = pl.BlockSpec((None, None, n_ctx, DH), lambda b, h, i: (b, layer, 0, h))

    def full(a):
        return pl.BlockSpec(a.shape, lambda b, h, i: (0,) * a.ndim)

    return pl.pallas_call(
        functools.partial(_lat_a_kernel, lam_init=lam_init, n_lat=n_lat),
        grid=(n_b, HEADS, nq),
        in_specs=[qcol(COL_AQ), qcol(COL_AG), tab, tab, kvs, kvs, cs, cs, full(l4), full(subg)],
        out_specs=pl.BlockSpec((LAT_TQ, DH), lambda b, h, i: (b * nq + i, h)),
        out_shape=jax.ShapeDtypeStruct((n_b * n_lat, W_BRANCH), BF16),
        scratch_shapes=[pltpu.VMEM((n_lat + n_ctx, DH), BF16)] * 2,
        compiler_params=_cparams(3),
        name="lat_a",
    )(proj, proj, rope_a[0], rope_a[1], akr, avb, cache_k, cache_v, l4, subg)


def _lat_b_kernel(q_ref, g_ref, cos, sin, k_ref, v_ref, kc_ref, vc_ref, qng, o_ref, kall, vall, *, n_lat):
    @pl.when(pl.program_id(2) == 0)
    def _():
        kall[0:n_lat, :] = k_ref[...]
        kall[n_lat:, :] = kc_ref[...].astype(BF16)
        vall[0:n_lat, :] = v_ref[...]
        vall[n_lat:, :] = vc_ref[...].astype(BF16)

    k = kall[...]
    v = vall[...]
    for g in range(GQA):
        sl = slice(g * DH, (g + 1) * DH)
        q = _rope(_rms(q_ref[:, sl]) * qng[...], cos[...], sin[...], DH // 4) * H_SCALE
        o = _softmax_pv([_dot_nt(q.astype(BF16), k)], [v])
        o_ref[:, sl] = (o * _silu(g_ref[:, sl])).astype(o_ref.dtype)


def _lat_b(proj, kvb, cache_k, cache_v, layer, rope_h, qng, n_b, n_lat):
    nq = n_lat // LAT_TQ
    n_ctx = cache_k.shape[2]
    gw = GQA * DH

    def qcol(c):
        return pl.BlockSpec((LAT_TQ, gw), lambda b, h, i, c=c: (b * nq + i, c // gw + h))

    tab = pl.BlockSpec((LAT_TQ, DH), lambda b, h, i: (i, 0))
    ks = pl.BlockSpec((n_lat, DH), lambda b, h, i: (b, h))
    vs = pl.BlockSpec((n_lat, DH), lambda b, h, i: (b, KV_HEADS + h))
    cs = pl.BlockSpec((None, None, n_ctx, DH), lambda b, h, i: (b, layer, 0, h))
    return pl.pallas_call(
        functools.partial(_lat_b_kernel, n_lat=n_lat),
        grid=(n_b, KV_HEADS, nq),
        in_specs=[qcol(COL_BQ), qcol(COL_BG), tab, tab, ks, vs, cs, cs,
                  pl.BlockSpec(qng.shape, lambda b, h, i: (0, 0))],
        out_specs=pl.BlockSpec((LAT_TQ, gw), lambda b, h, i: (b * nq + i, h)),
        out_shape=jax.ShapeDtypeStruct((n_b * n_lat, W_BRANCH), BF16),
        scratch_shapes=[pltpu.VMEM((n_lat + n_ctx, DH), BF16)] * 2,
        compiler_params=_cparams(3),
        name="lat_b",
    )(proj, proj, rope_h[0], rope_h[1], kvb, kvb, cache_k, cache_v, qng)


WIN_TQ = 512


def _lat_d_kernel(q_ref, g_ref, cos, sin, k_ref, v_ref, kc_ref, vc_ref, sink, o_ref, kpad, vpad, *, n_lat):
    qi = pl.program_id(2)
    kvh = pl.program_id(1)

    @pl.when(qi == 0)
    def _():
        zeros = jnp.zeros((WINDOW, DH), BF16)
        kpad[0:WINDOW, :] = zeros
        kpad[WINDOW:WINDOW + n_lat, :] = k_ref[...]
        kpad[WINDOW + n_lat:, :] = zeros
        vpad[0:WINDOW, :] = zeros
        vpad[WINDOW:WINDOW + n_lat, :] = v_ref[...]
        vpad[WINDOW + n_lat:, :] = zeros

    span = WIN_TQ + 2 * WINDOW
    start = pl.multiple_of(qi * WIN_TQ, WIN_TQ)
    kw = kpad[pl.ds(start, span), :]
    vw = vpad[pl.ds(start, span), :]
    kc = kc_ref[...].astype(BF16)
    vc = vc_ref[...].astype(BF16)
    k_pos = start - WINDOW + lax.broadcasted_iota(jnp.int32, (1, span), 1)
    q_pos = start + lax.broadcasted_iota(jnp.int32, (WIN_TQ, 1), 0)
    band = (k_pos >= 0) & (k_pos < n_lat) & (jnp.abs(q_pos - k_pos) <= WINDOW)
    lane = lax.broadcasted_iota(jnp.int32, sink.shape, 1)
    sink_row = sink[...]
    for g in range(GQA):
        sl = slice(g * DH, (g + 1) * DH)
        q = (_rope(q_ref[:, sl], cos[...], sin[...], DH // 4) * H_SCALE).astype(BF16)
        s_w = jnp.where(band, _dot_nt(q, kw), NEG_INF)
        s_c = _dot_nt(q, kc)
        sk = jnp.sum(jnp.where(lane == kvh * GQA + g, sink_row, 0.0), axis=-1, keepdims=True)
        o = _softmax_pv([s_w, s_c], [vw, vc], sink=sk)
        o_ref[:, sl] = (o * _silu(g_ref[:, sl])).astype(o_ref.dtype)


def _lat_d(proj, kvb, cache_k, cache_v, layer, rope_h, sink, n_b, n_lat):
    nq = n_lat // WIN_TQ
    n_ctx = cache_k.shape[2]
    gw = GQA * DH

    def qcol(c):
        return pl.BlockSpec((WIN_TQ, gw), lambda b, h, i, c=c: (b * nq + i, c // gw + h))

    tab = pl.BlockSpec((WIN_TQ, DH), lambda b, h, i: (i, 0))
    ks = pl.BlockSpec((n_lat, DH), lambda b, h, i: (b, 2 * KV_HEADS + h))
    vs = pl.BlockSpec((n_lat, DH), lambda b, h, i: (b, 3 * KV_HEADS + h))
    cs = pl.BlockSpec((None, None, n_ctx, DH), lambda b, h, i: (b, layer, 0, h))
    return pl.pallas_call(
        functools.partial(_lat_d_kernel, n_lat=n_lat),
        grid=(n_b, KV_HEADS, nq),
        in_specs=[qcol(COL_DQ), qcol(COL_DG), tab, tab, ks, vs, cs, cs,
                  pl.BlockSpec(sink.shape, lambda b, h, i: (0, 0))],
        out_specs=pl.BlockSpec((WIN_TQ, gw), lambda b, h, i: (b * nq + i, h)),
        out_shape=jax.ShapeDtypeStruct((n_b * n_lat, W_BRANCH), BF16),
        scratch_shapes=[pltpu.VMEM((n_lat + 2 * WINDOW, DH), BF16)] * 2,
        compiler_params=_cparams(3),
        name="lat_d",
    )(proj, proj, rope_h[0], rope_h[1], kvb, kvb, cache_k, cache_v, sink)


def _s5_disc_kernel(lre, lim, ldt, are, aim, fre, fim):
    lam_re = lre[...]
    lam_im = lim[...]
    dt = jnp.exp(ldt[...])
    mag = jnp.exp(lam_re * dt)
    a_re = mag * jnp.cos(lam_im * dt)
    a_im = mag * jnp.sin(lam_im * dt)
    den = lam_re * lam_re + lam_im * lam_im
    nr = a_re - 1.0
    are[...] = a_re
    aim[...] = a_im
    fre[...] = (nr * lam_re + a_im * lam_im) / den
    fim[...] = (a_im * lam_re - nr * lam_im) / den


def _s5_disc(lam_re, lam_im, log_dt):
    r = lam_re.shape[0]
    spec = pl.BlockSpec((r, C_STATE), lambda: (0, 0))
    return pl.pallas_call(
        _s5_disc_kernel,
        in_specs=[spec, spec, pl.BlockSpec((r, 1), lambda: (0, 0))],
        out_specs=[spec] * 4,
        out_shape=[jax.ShapeDtypeStruct((r, C_STATE), F32)] * 4,
        name="s5_disc",
    )(lam_re, lam_im, log_dt)


S5_MM_ROWS = 512


def _s5_kernel(u_ref, r_ref, cm_ref, a_ref, h0_ref, d_ref, y_ref, fin_ref, s_ref, *, steps, segmented):
    sl_re = slice(0, STATE_LANES)
    sl_im = slice(STATE_LANES, 2 * STATE_LANES)
    n_rows = steps * CHAINS
    n_mm = n_rows // S5_MM_ROWS

    def in_proj(d):
        def body(c, carry):
            r0 = pl.multiple_of(c * S5_MM_ROWS, S5_MM_ROWS)
            s_ref[pl.ds(r0, S5_MM_ROWS), :] = _dot(u_ref[pl.ds(r0, S5_MM_ROWS), :].astype(BF16), r_ref[d])
            return carry
        lax.fori_loop(0, n_mm, body, 0)

    def out_proj(d):
        def body(c, carry):
            r0 = pl.multiple_of(c * S5_MM_ROWS, S5_MM_ROWS)
            y = _dot(s_ref[pl.ds(r0, S5_MM_ROWS), :].astype(BF16), cm_ref[d])
            if d == 0:
                y_ref[pl.ds(r0, S5_MM_ROWS), :] = y + u_ref[pl.ds(r0, S5_MM_ROWS), :] * d_ref[...]
            else:
                y_ref[pl.ds(r0, S5_MM_ROWS), :] += y
            return carry
        lax.fori_loop(0, n_mm, body, 0)

    def scan(d, h_re, h_im, store):
        a_re = jnp.broadcast_to(a_ref[d, 0], (CHAINS, STATE_LANES))
        a_im = jnp.broadcast_to(a_ref[d, 1], (CHAINS, STATE_LANES))

        def body(t, carry):
            hr, hi = carry
            k = t if d == 0 else steps - 1 - t
            r0 = pl.multiple_of(k * CHAINS, CHAINS)
            nr = a_re * hr - a_im * hi + s_ref[pl.ds(r0, CHAINS), sl_re]
            ni = a_re * hi + a_im * hr + s_ref[pl.ds(r0, CHAINS), sl_im]
            if store:
                s_ref[pl.ds(r0, CHAINS), sl_re] = nr
                s_ref[pl.ds(r0, CHAINS), sl_im] = ni
            return nr, ni

        return lax.fori_loop(0, steps, body, (h_re, h_im), unroll=4)

    def segment_starts(d, e_re, e_im, h0_re, h0_im):
        p_re = a_ref[d, 0]
        p_im = a_ref[d, 1]
        n_sq = int(round(math.log2(steps)))
        assert 2 ** n_sq == steps
        for _ in range(n_sq):
            p_re, p_im = p_re * p_re - p_im * p_im, 2.0 * p_re * p_im
        order = list(range(CHAINS)) if d == 0 else list(range(CHAINS - 1, -1, -1))
        first = order[0]
        cur_re = h0_re[first:first + 1]
        cur_im = h0_im[first:first + 1]
        rows_re = {first: cur_re}
        rows_im = {first: cur_im}
        for prev, nxt in zip(order[:-1], order[1:]):
            nre = e_re[prev:prev + 1] + p_re * cur_re - p_im * cur_im
            nim = e_im[prev:prev + 1] + p_re * cur_im + p_im * cur_re
            cur_re, cur_im = nre, nim
            rows_re[nxt] = cur_re
            rows_im[nxt] = cur_im
        row_id = lax.broadcasted_iota(jnp.int32, (CHAINS, STATE_LANES), 0)
        s_re = jnp.zeros((CHAINS, STATE_LANES), F32)
        s_im = jnp.zeros((CHAINS, STATE_LANES), F32)
        for i in range(CHAINS):
            s_re = jnp.where(row_id == i, rows_re[i], s_re)
            s_im = jnp.where(row_id == i, rows_im[i], s_im)
        return s_re, s_im

    for d in range(2):
        in_proj(d)
        h_re = h0_ref[d, 0]
        h_im = h0_ref[d, 1]
        if segmented:
            zero = jnp.zeros((CHAINS, STATE_LANES), F32)
            e_re, e_im = scan(d, zero, zero, store=False)
            h_re, h_im = segment_starts(d, e_re, e_im, h_re, h_im)
        f_re, f_im = scan(d, h_re, h_im, store=True)
        fin_ref[d, 0] = f_re
        fin_ref[d, 1] = f_im
        out_proj(d)


def _s5_scan(u_r, rmat, cmat, a_blk, h0, d_skip, steps, segmented):
    n_g = u_r.shape[0]
    rows = steps * CHAINS
    gl = GROUPS_PER_BLOCK * C_CH
    h0_idx = (lambda g, j: (g, 0, 0, 0, j)) if h0.shape[0] > 1 else (lambda g, j: (0, 0, 0, 0, j))
    return pl.pallas_call(
        functools.partial(_s5_kernel, steps=steps, segmented=segmented),
        grid=(n_g, N_GBLK),
        in_specs=[pl.BlockSpec((None, rows, gl), lambda g, j: (g, 0, j)),
                  pl.BlockSpec((2, None, gl, 2 * STATE_LANES), lambda g, j: (0, j, 0, 0)),
                  pl.BlockSpec((2, None, 2 * STATE_LANES, gl), lambda g, j: (0, j, 0, 0)),
                  pl.BlockSpec((2, 2, 1, STATE_LANES), lambda g, j: (0, 0, 0, j)),
                  pl.BlockSpec((None, 2, 2, CHAINS, STATE_LANES), h0_idx),
                  pl.BlockSpec((1, gl), lambda g, j: (0, j))],
        out_specs=[pl.BlockSpec((None, rows, gl), lambda g, j: (g, 0, j)),
                   pl.BlockSpec((None, 2, 2, CHAINS, STATE_LANES), lambda g, j: (g, 0, 0, 0, j))],
        out_shape=[jax.ShapeDtypeStruct((n_g, rows, W_BRANCH), F32),
                   jax.ShapeDtypeStruct((n_g, 2, 2, CHAINS, C_GROUPS * C_STATE), F32)],
        scratch_shapes=[pltpu.VMEM((rows, 2 * STATE_LANES), F32)],
        compiler_params=_cparams(2),
        name="s5_scan",
    )(u_r, rmat, cmat, a_blk, h0, d_skip)


POST_TM = 512


def _s5_post_kernel(y_ref, g_ref, w_ref, b_ref, o_ref):
    y = jax.nn.gelu(y_ref[...])
    z = _dot(y.astype(BF16), w_ref[...]) + b_ref[...]
    o_ref[...] = (y * jax.nn.sigmoid(z) * _silu(g_ref[...])).astype(o_ref.dtype)


def _s5_post(y_pre, proj, glu_w_bf16, glu_b):
    m = y_pre.shape[0]
    wb = W_BRANCH
    return pl.pallas_call(
        _s5_post_kernel,
        grid=(m // POST_TM,),
        in_specs=[pl.BlockSpec((POST_TM, wb), lambda i: (i, 0)),
                  pl.BlockSpec((POST_TM, wb), lambda i: (i, COL_CG // wb)),
                  pl.BlockSpec((wb, wb), lambda i: (0, 0)),
                  pl.BlockSpec((1, wb), lambda i: (0, 0))],
        out_specs=pl.BlockSpec((POST_TM, wb), lambda i: (i, 0)),
        out_shape=jax.ShapeDtypeStruct((m, wb), BF16),
        compiler_params=_cparams(1),
        name="s5_post",
    )(y_pre, proj, glu_w_bf16, glu_b)


OUT_TM = 512
OUT_TK = 512
OUT_KSTEPS = W_BRANCH // OUT_TK


def _wout_kernel(a_ref, b_ref, c_ref, d_ref, w_ref, x_ref, gate_ref, lg_ref, lb_ref, o_ref):
    k = pl.program_id(1)
    for i, ref in enumerate((a_ref, b_ref, c_ref, d_ref)):
        @pl.when(k // OUT_KSTEPS == i)
        def _(ref=ref, i=i):
            prod = _dot(ref[...], w_ref[...])
            if i == 0:
                @pl.when(k == 0)
                def _():
                    o_ref[...] = prod

                @pl.when(k != 0)
                def _():
                    o_ref[...] += prod
            else:
                o_ref[...] += prod

    @pl.when(k == pl.num_programs(1) - 1)
    def _():
        gate = gate_ref[...]
        lg = lg_ref[...]
        lb = lb_ref[...]

        def body(i, carry):
            r0 = pl.multiple_of(i * LN_ROWS, LN_ROWS)
            z = DEEPNORM_ALPHA * x_ref[pl.ds(r0, LN_ROWS), :] + gate * o_ref[pl.ds(r0, LN_ROWS), :]
            mu = jnp.mean(z, axis=-1, keepdims=True)
            zc = z - mu
            var = jnp.mean(zc * zc, axis=-1, keepdims=True)
            o_ref[pl.ds(r0, LN_ROWS), :] = zc * lax.rsqrt(var + LN_EPS) * lg + lb
            return carry

        lax.fori_loop(0, o_ref.shape[0] // LN_ROWS, body, 0)


def _wout(branches, w_out_bf16, x, ada_l, row_of_tile, ln_g, ln_b):
    m, d = x.shape

    def branch_spec(i):
        return pl.BlockSpec((OUT_TM, OUT_TK),
                            lambda mi, k, i=i: (mi, jnp.clip(k - OUT_KSTEPS * i, 0, OUT_KSTEPS - 1)))

    return pl.pallas_call(
        _wout_kernel,
        grid=(m // OUT_TM, 4 * OUT_KSTEPS),
        in_specs=[branch_spec(0), branch_spec(1), branch_spec(2), branch_spec(3),
                  pl.BlockSpec((OUT_TK, d), lambda mi, k: (k, 0)),
                  pl.BlockSpec((OUT_TM, d), lambda mi, k: (mi, 0)),
                  pl.BlockSpec((None, 1, d), lambda mi, k: (row_of_tile(mi), 0, 2)),
                  pl.BlockSpec((1, d), lambda mi, k: (0, 0)),
                  pl.BlockSpec((1, d), lambda mi, k: (0, 0))],
        out_specs=pl.BlockSpec((OUT_TM, d), lambda mi, k: (mi, 0)),
        out_shape=jax.ShapeDtypeStruct((m, d), F32),
        compiler_params=_cparams(2),
        name="wout",
    )(*branches, w_out_bf16, x, ada_l, ln_g, ln_b)


def _rope_tables(n_lat, d, tile):
    n_rows = n_lat // GRID_W
    rows = jnp.repeat(jnp.arange(n_rows, dtype=F32), GRID_W)
    cols = jnp.tile(jnp.arange(GRID_W, dtype=F32), n_rows)
    n_freq = d // 4
    inv = ROPE_THETA ** (-jnp.arange(n_freq, dtype=F32) / n_freq)
    ar = rows[:, None] * inv
    ac = cols[:, None] * inv
    ang = jnp.concatenate([ar, ar, ac, ac], axis=-1)
    sign = jnp.concatenate([-jnp.ones((n_freq,), F32), jnp.ones((n_freq,), F32)] * 2)
    cos = jnp.tile(jnp.cos(ang), (1, tile))
    sin = jnp.tile(jnp.sin(ang) * sign, (1, tile))
    return cos, sin


def _block_diag(per_group):
    eye = jnp.eye(GROUPS_PER_BLOCK, dtype=per_group.dtype)
    nb, g, r, c = per_group.shape
    return jnp.einsum('jgrc,gh->jgrhc', per_group, eye).reshape(nb, g * r, g * c)


def _to_chain_rows(u, n_outer, steps):
    w = u.shape[-1]
    return u.reshape(n_outer, CHAINS, steps, w).transpose(0, 2, 1, 3).reshape(n_outer, steps * CHAINS, w)


def _from_chain_rows(y, n_outer, steps):
    w = y.shape[-1]
    return y.reshape(n_outer, steps, CHAINS, w).transpose(0, 2, 1, 3).reshape(n_outer * CHAINS * steps, w)


def kernel(x_prompt, x_sample, c, cache_a_k, cache_a_v, cache_b_k, cache_b_v, cache_d_k, cache_d_v,
           state_c_re, state_c_im, c_ctx, w_ada, b_ada, w_in, w_out, ln_g, ln_b,
           a_lam_q1, a_lam_k1, a_lam_q2, a_lam_k2, a_subln_g, b_qnorm_g, b_knorm_g, d_sink,
           c_lam_re, c_lam_im, c_log_dt, c_b_re, c_b_im, c_c_re, c_c_im, c_d, c_glu_w, c_glu_b):
    n_b_ctx, seq, d_model = x_prompt.shape
    n_b, n_lat, _ = x_sample.shape
    n_ctx = cache_a_k.shape[2]
    depth = w_in.shape[0]
    lam_inits = [0.8 - 0.6 * math.exp(-0.3 * l) for l in range(depth)]

    cond = jnp.concatenate([c_ctx[None], c], axis=0)
    cond_cols = jnp.zeros((d_model, 8), F32).at[:, :N_COND].set(cond.T)
    ada = _ada(cond_cols, w_ada, b_ada).reshape(depth, 8, 1, 3 * d_model)

    bounds = np.concatenate([[0], np.cumsum(IN_SIZES)])
    w_in_r = jnp.concatenate([w_in[:, :, bounds[s]:bounds[s + 1]] for s in _SEG_ORDER], axis=-1).astype(BF16)
    w_out_b = w_out.astype(BF16)
    glu_w_b = c_glu_w.astype(BF16)

    nrow = depth * 2 * C_GROUPS
    a_re, a_im, f_re, f_im = _s5_disc(c_lam_re.reshape(nrow, C_STATE), c_lam_im.reshape(nrow, C_STATE),
                                      c_log_dt.reshape(nrow, 1))
    shp = (depth, 2, C_GROUPS, C_STATE)
    f_re = f_re.reshape(shp)[..., None]
    f_im = f_im.reshape(shp)[..., None]
    bb_re = f_re * c_b_re - f_im * c_b_im
    bb_im = f_re * c_b_im + f_im * c_b_re

    def grp(t):
        return t.reshape(depth * 2 * N_GBLK, GROUPS_PER_BLOCK, t.shape[-2], t.shape[-1])

    r_re = _block_diag(grp(jnp.swapaxes(bb_re, -1, -2)))
    r_im = _block_diag(grp(jnp.swapaxes(bb_im, -1, -2)))
    gl = GROUPS_PER_BLOCK * C_CH
    rmat = jnp.concatenate([r_re, r_im], axis=-1).reshape(depth, 2, N_GBLK, gl, 2 * STATE_LANES).astype(BF16)
    cm_re = _block_diag(grp(jnp.swapaxes(c_c_re, -1, -2)))
    cm_im = _block_diag(grp(jnp.swapaxes(c_c_im, -1, -2)))
    cmat = jnp.concatenate([cm_re, -cm_im], axis=-2).reshape(depth, 2, N_GBLK, 2 * STATE_LANES, gl).astype(BF16)
    a_blk = jnp.stack([a_re.reshape(depth, 2, 1, C_GROUPS * C_STATE),
                       a_im.reshape(depth, 2, 1, C_GROUPS * C_STATE)], axis=2)

    rope_a = _rope_tables(n_lat, A_DK, DH // A_DK)
    rope_h = _rope_tables(n_lat, DH, 1)

    ctx_steps = seq
    n_ctx_groups = n_b_ctx // CHAINS
    lat_steps = n_lat // CHAINS
    h0_ctx = jnp.zeros((1, 2, 2, CHAINS, C_GROUPS * C_STATE), F32)

    ctx_tiles_in = lambda i: 0
    lat_row_in = lambda i: 1 + i // (n_lat // WIN_TM)
    lat_row_out = lambda i: 1 + i // (n_lat // OUT_TM)

    xc = x_prompt.reshape(n_b_ctx * seq, d_model)
    xl = x_sample.reshape(n_b * n_lat, d_model)
    new = {k: [] for k in ("ak", "av", "bk", "bv", "dk", "dv", "sre", "sim")}

    for l in range(depth):
        ada_l = ada[l]
        l4 = jnp.stack([a_lam_q1[l], a_lam_k1[l], a_lam_q2[l], a_lam_k2[l]], axis=0)
        subg = a_subln_g[l][None]
        qng = b_qnorm_g[l][None]
        kng = b_knorm_g[l][None]
        sink = d_sink[l][None]
        d_skip = c_d[l][None]
        glu_b = c_glu_b[l][None]

        proj = _ln_win(xc, ada_l, w_in_r[l], ctx_tiles_in)
        oa, ob, od, bk = _ctx_attn(proj, seq, l4, subg, qng, kng, sink, lam_inits[l])
        u_r = _to_chain_rows(proj[:, COL_CU:COL_CU + W_BRANCH], n_ctx_groups, ctx_steps)
        y_pre, fin = _s5_scan(u_r, rmat[l], cmat[l], a_blk[l], h0_ctx, d_skip, ctx_steps, segmented=False)
        oc = _s5_post(_from_chain_rows(y_pre, n_ctx_groups, ctx_steps), proj, glu_w_b[l], glu_b)
        xc_new = _wout((oa, ob, oc, od), w_out_b[l], xc, ada_l, ctx_tiles_in, ln_g[l][None], ln_b[l][None])

        kvw = KV_HEADS * DH
        new["ak"].append(proj[:, COL_AK:COL_AK + W_BRANCH].reshape(n_b_ctx, seq, HEADS, DH))
        new["av"].append(proj[:, COL_AV:COL_AV + W_BRANCH].reshape(n_b_ctx, seq, HEADS, DH))
        new["bk"].append(bk.reshape(n_b_ctx, seq, KV_HEADS, DH))
        new["bv"].append(proj[:, COL_KV + kvw:COL_KV + 2 * kvw].reshape(n_b_ctx, seq, KV_HEADS, DH))
        new["dk"].append(proj[:, COL_KV + 2 * kvw:COL_KV + 3 * kvw].reshape(n_b_ctx, seq, KV_HEADS, DH))
        new["dv"].append(proj[:, COL_KV + 3 * kvw:COL_KV + 4 * kvw].reshape(n_b_ctx, seq, KV_HEADS, DH))
        fin_b = fin.transpose(0, 3, 1, 2, 4).reshape(n_b_ctx, 2, 2, C_GROUPS, C_STATE)
        new["sre"].append(fin_b[:, :, 0])
        new["sim"].append(fin_b[:, :, 1])

        proj = _ln_win(xl, ada_l, w_in_r[l], lat_row_in)
        akr, avb, kvb = _kv_prep(proj, n_lat, rope_a, rope_h, kng)
        ca_k = cache_a_k.reshape(n_b, depth, n_ctx, HEADS * DH)
        ca_v = cache_a_v.reshape(n_b, depth, n_ctx, HEADS * DH)
        cb_k = cache_b_k.reshape(n_b, depth, n_ctx, KV_HEADS * DH)
        cb_v = cache_b_v.reshape(n_b, depth, n_ctx, KV_HEADS * DH)
        cd_k = cache_d_k.reshape(n_b, depth, n_ctx, KV_HEADS * DH)
        cd_v = cache_d_v.reshape(n_b, depth, n_ctx, KV_HEADS * DH)
        oa = _lat_a(proj, akr, avb, ca_k, ca_v, l, rope_a, l4, subg, lam_inits[l], n_b, n_lat)
        ob = _lat_b(proj, kvb, cb_k, cb_v, l, rope_h, qng, n_b, n_lat)
        od = _lat_d(proj, kvb, cd_k, cd_v, l, rope_h, sink, n_b, n_lat)

        u_r = _to_chain_rows(proj[:, COL_CU:COL_CU + W_BRANCH], n_b, lat_steps)
        s_re = state_c_re[:, l].reshape(n_b, 2, C_GROUPS * C_STATE)
        s_im = state_c_im[:, l].reshape(n_b, 2, C_GROUPS * C_STATE)
        h0 = jnp.zeros((n_b, 2, 2, CHAINS, C_GROUPS * C_STATE), F32)
        h0 = h0.at[:, 0, 0, 0].set(s_re[:, 0]).at[:, 0, 1, 0].set(s_im[:, 0])
        h0 = h0.at[:, 1, 0, CHAINS - 1].set(s_re[:, 1]).at[:, 1, 1, CHAINS - 1].set(s_im[:, 1])
        y_pre, _ = _s5_scan(u_r, rmat[l], cmat[l], a_blk[l], h0, d_skip, lat_steps, segmented=True)
        oc = _s5_post(_from_chain_rows(y_pre, n_b, lat_steps), proj, glu_w_b[l], glu_b)
        xl_new = _wout((oa, ob, oc, od), w_out_b[l], xl, ada_l, lat_row_out, ln_g[l][None], ln_b[l][None])

        xc, xl = xc_new, xl_new

    stk = lambda k: jnp.stack(new[k], axis=1)
    return (xc.reshape(n_b_ctx, seq, d_model), xl.reshape(n_b, n_lat, d_model),
            stk("ak"), stk("av"), stk("bk"), stk("bv"), stk("dk"), stk("dv"), stk("sre"), stk("sim"))
```

```python
import functools
import math

import jax
import jax.numpy as jnp
import numpy as np
from jax import lax
from jax.experimental import pallas as pl
from jax.experimental.pallas import tpu as pltpu

F32 = jnp.float32
BF16 = jnp.bfloat16

D_MODEL = 4096
DEPTH = 2
GRID_W = 64
ROPE_THETA = 10000.0
LN_EPS = 1e-5
RMS_EPS = 1e-6
NEG_INF = -1e30
WINDOW = 128

W_BRANCH = D_MODEL // 4
HEADS = 8
KV_HEADS = 2
GQA = HEADS // KV_HEADS
DH = W_BRANCH // HEADS
A_DK = DH // 2
C_CH = 16
C_GROUPS = W_BRANCH // C_CH
C_STATE = 64
GROUPS_PER_BLOCK = 8
N_GBLK = C_GROUPS // GROUPS_PER_BLOCK
STATE_LANES = GROUPS_PER_BLOCK * C_STATE
CHAINS = 8

IN_SIZES = (W_BRANCH, W_BRANCH, W_BRANCH, W_BRANCH,
            W_BRANCH, KV_HEADS * DH, KV_HEADS * DH, W_BRANCH,
            W_BRANCH, W_BRANCH,
            W_BRANCH, KV_HEADS * DH, KV_HEADS * DH, W_BRANCH)
D_IN = sum(IN_SIZES)
_SEG_ORDER = (0, 1, 2, 3, 4, 7, 8, 9, 10, 13, 5, 6, 11, 12)
COL_AQ, COL_AK, COL_AV, COL_AG = 0, 1024, 2048, 3072
COL_BQ, COL_BG = 4096, 5120
COL_CU, COL_CG = 6144, 7168
COL_DQ, COL_DG = 8192, 9216
COL_KV = 10240
DEEPNORM_ALPHA = (2 * DEPTH) ** 0.25

A_SCALE = A_DK ** -0.5
H_SCALE = DH ** -0.5
LOG2_E = math.log2(math.e)

VMEM_LIMIT = 56 * 1024 * 1024


def _cparams(n_axes):
    return pltpu.CompilerParams(dimension_semantics=("arbitrary",) * n_axes,
                                vmem_limit_bytes=VMEM_LIMIT)


def _dot(a, b):
    return jnp.dot(a, b, preferred_element_type=F32)


def _dot_nt(a, b):
    return lax.dot_general(a, b, (((1,), (1,)), ((), ())), preferred_element_type=F32)


def _silu(x):
    return x * jax.nn.sigmoid(x)


def _rms(x):
    return x * lax.rsqrt(jnp.mean(x * x, axis=-1, keepdims=True) + RMS_EPS)


def _rope(x, cos, sin_signed, quarter):
    lane = lax.broadcasted_iota(jnp.int32, (1, x.shape[-1]), 1)
    first = (lane % (2 * quarter)) < quarter
    rot = jnp.where(first, pltpu.roll(x, x.shape[-1] - quarter, 1), pltpu.roll(x, quarter, 1))
    return x * cos + rot * sin_signed


N_COND = 3
ADA_TN = 512


def _ada_kernel(c_ref, w_ref, b_ref, o_ref, sc_ref):
    @pl.when((pl.program_id(0) == 0) & (pl.program_id(1) == 0))
    def _():
        sc_ref[...] = _silu(c_ref[...])

    tn = w_ref.shape[1]

    def body(i, accs):
        r0 = pl.multiple_of(i * 8, 8)
        w = w_ref[pl.ds(r0, 8), :]
        out = []
        for r, acc in enumerate(accs):
            cs = sc_ref[r, pl.ds(r0, 8), :]
            out.append(acc + w * jnp.concatenate([cs] * (tn // cs.shape[1]), axis=1))
        return tuple(out)

    accs = lax.fori_loop(0, w_ref.shape[0] // 8, body,
                         tuple(jnp.zeros((8, tn), F32) for _ in range(N_COND)), unroll=4)
    row_id = lax.broadcasted_iota(jnp.int32, (8, tn), 0)
    out = jnp.zeros((8, tn), F32)
    for r, a in enumerate(accs):
        out = jnp.where(row_id == r, jnp.sum(a, axis=0, keepdims=True), out)
    o_ref[...] = out + b_ref[...]


def _ada(cond_lanes, w_ada, b_ada):
    depth, d, n = w_ada.shape
    return pl.pallas_call(
        _ada_kernel,
        grid=(depth, n // ADA_TN),
        in_specs=[pl.BlockSpec(cond_lanes.shape, lambda l, j: (0, 0, 0)),
                  pl.BlockSpec((None, d, ADA_TN), lambda l, j: (l, 0, j)),
                  pl.BlockSpec((None, 1, ADA_TN), lambda l, j: (l, 0, j))],
        out_specs=pl.BlockSpec((None, 8, ADA_TN), lambda l, j: (l, 0, j)),
        out_shape=jax.ShapeDtypeStruct((depth, 8, n), F32),
        scratch_shapes=[pltpu.VMEM(cond_lanes.shape, F32)],
        compiler_params=_cparams(2),
        name="ada",
    )(cond_lanes, w_ada, b_ada.reshape(depth, 1, n))


LN_TM = 512
LN_ROWS = 32
WIN_TM = 1024
WIN_TN = 1024


def _ln_mod_kernel(x_ref, sh_ref, sc_ref, h_ref):
    shift = sh_ref[...]
    scale1 = 1.0 + sc_ref[...]

    def body(i, carry):
        r0 = pl.multiple_of(i * LN_ROWS, LN_ROWS)
        x = x_ref[pl.ds(r0, LN_ROWS), :]
        mu = jnp.mean(x, axis=-1, keepdims=True)
        xc = x - mu
        var = jnp.mean(xc * xc, axis=-1, keepdims=True)
        h = xc * lax.rsqrt(var + LN_EPS) * scale1 + shift
        h_ref[pl.ds(r0, LN_ROWS), :] = h.astype(BF16)
        return carry

    lax.fori_loop(0, x_ref.shape[0] // LN_ROWS, body, 0)


def _win_kernel(h_ref, w_ref, o_ref):
    o_ref[...] = _dot(h_ref[...], w_ref[...]).astype(o_ref.dtype)


def _ln_win(x, ada_l, w_in_bf16, row_of_tile):
    m, d = x.shape
    n = w_in_bf16.shape[1]
    h = pl.pallas_call(
        _ln_mod_kernel,
        grid=(m // LN_TM,),
        in_specs=[pl.BlockSpec((LN_TM, d), lambda i: (i, 0)),
                  pl.BlockSpec((None, 1, d), lambda i: (row_of_tile(i), 0, 0)),
                  pl.BlockSpec((None, 1, d), lambda i: (row_of_tile(i), 0, 1))],
        out_specs=pl.BlockSpec((LN_TM, d), lambda i: (i, 0)),
        out_shape=jax.ShapeDtypeStruct((m, d), BF16),
        compiler_params=_cparams(1),
        name="ln_mod",
    )(x, ada_l, ada_l)
    return pl.pallas_call(
        _win_kernel,
        grid=(m // WIN_TM, n // WIN_TN),
        in_specs=[pl.BlockSpec((WIN_TM, d), lambda i, j: (i, 0)),
                  pl.BlockSpec((d, WIN_TN), lambda i, j: (0, j))],
        out_specs=pl.BlockSpec((WIN_TM, WIN_TN), lambda i, j: (i, j)),
        out_shape=jax.ShapeDtypeStruct((m, n), F32),
        compiler_params=_cparams(2),
        name="win_mm",
    )(h, w_in_bf16)


def _diff_lambda(l4, lam_init):
    e1 = jnp.exp(jnp.sum(l4[0:1] * l4[1:2], axis=-1, keepdims=True))
    e2 = jnp.exp(jnp.sum(l4[2:3] * l4[3:4], axis=-1, keepdims=True))
    return e1 - e2 + lam_init


def _split_maps(q):
    lane = lax.broadcasted_iota(jnp.int32, (1, DH), 1)
    first = lane < A_DK
    return jnp.where(first, q, 0.0).astype(BF16), jnp.where(first, 0.0, q).astype(BF16)


def _softmax_pv(s_list, v_list, sink=None):
    m = s_list[0].max(axis=-1, keepdims=True)
    for s in s_list[1:]:
        m = jnp.maximum(m, s.max(axis=-1, keepdims=True))
    if sink is not None:
        m = jnp.maximum(m, sink)
    den = None
    o = None
    for s, v in zip(s_list, v_list):
        p = jnp.exp(s - m)
        ps = p.sum(axis=-1, keepdims=True)
        den = ps if den is None else den + ps
        pv = _dot(p.astype(BF16), v)
        o = pv if o is None else o + pv
    if sink is not None:
        den = den + jnp.exp(sink - m)
    return o / den


def _diff_finish(o0, o1, lam, lam_init, subg, gate):
    o = o0 - lam * o1
    return _rms(o) * subg * (1.0 - lam_init) * _silu(gate)


def _ctx_attn_kernel(aq, ak, av, ag, bq, bg, dq, dg, kv, l4, subg, qng, kng, sink, *rest, lam_init):
    oa, ob, od, c_ak, c_av, c_bk, c_bv, c_dk, c_dv = rest[-9:]
    seq = aq.shape[0]
    lam = _diff_lambda(l4[...], lam_init)
    for h in range(HEADS):
        sl = slice(h * DH, (h + 1) * DH)
        q0, q1 = _split_maps(aq[:, sl] * A_SCALE)
        k32 = ak[:, sl]
        v32 = av[:, sl]
        c_ak[pl.ds(h, seq, stride=HEADS), :] = k32
        c_av[pl.ds(h, seq, stride=HEADS), :] = v32
        k = k32.astype(BF16)
        v = v32.astype(BF16)
        o0 = _softmax_pv([_dot_nt(q0, k)], [v])
        o1 = _softmax_pv([_dot_nt(q1, k)], [v])
        oa[:, sl] = _diff_finish(o0, o1, lam, lam_init, subg[...], ag[:, sl]).astype(oa.dtype)

    kvw = KV_HEADS * DH
    for kvh in range(KV_HEADS):
        ksl = slice(kvh * DH, (kvh + 1) * DH)
        bk = _rms(kv[:, ksl]) * kng[...]
        bv = kv[:, kvw + kvh * DH: kvw + (kvh + 1) * DH]
        dk = kv[:, 2 * kvw + kvh * DH: 2 * kvw + (kvh + 1) * DH]
        dv = kv[:, 3 * kvw + kvh * DH: 3 * kvw + (kvh + 1) * DH]
        for c_ref, val in ((c_bk, bk), (c_bv, bv), (c_dk, dk), (c_dv, dv)):
            c_ref[pl.ds(kvh, seq, stride=KV_HEADS), :] = val
        bkb = bk.astype(BF16)
        bvb = bv.astype(BF16)
        dkb = dk.astype(BF16)
        dvb = dv.astype(BF16)
        for g in range(GQA):
            h = kvh * GQA + g
            sl = slice(h * DH, (h + 1) * DH)
            q = (_rms(bq[:, sl]) * qng[...] * H_SCALE).astype(BF16)
            o = _softmax_pv([_dot_nt(q, bkb)], [bvb])
            ob[:, sl] = (o * _silu(bg[:, sl])).astype(ob.dtype)
            q = (dq[:, sl] * H_SCALE).astype(BF16)
            o = _softmax_pv([_dot_nt(q, dkb)], [dvb], sink=sink[:, h:h + 1])
            od[:, sl] = (o * _silu(dg[:, sl])).astype(od.dtype)


def _ctx_attn(proj, seq, l4, subg, qng, kng, sink, lam_init, layer, depth, caches):
    m = proj.shape[0]
    n_b = m // seq
    wb = W_BRANCH

    def col(c):
        return pl.BlockSpec((seq, wb), lambda b, c=c: (b, c // wb))

    def full(a):
        return pl.BlockSpec(a.shape, lambda b: (0,) * a.ndim)

    cache_heads = (HEADS, HEADS, KV_HEADS, KV_HEADS, KV_HEADS, KV_HEADS)
    n_in = 14
    alias_specs = [] if caches is None else [pl.BlockSpec(memory_space=pl.ANY)] * 6
    aliases = {} if caches is None else {n_in + i: 3 + i for i in range(6)}
    outs = pl.pallas_call(
        functools.partial(_ctx_attn_kernel, lam_init=lam_init),
        grid=(n_b,),
        in_specs=[col(COL_AQ), col(COL_AK), col(COL_AV), col(COL_AG), col(COL_BQ), col(COL_BG),
                  col(COL_DQ), col(COL_DG), col(COL_KV),
                  full(l4), full(subg), full(qng), full(kng), full(sink)] + alias_specs,
        out_specs=[pl.BlockSpec((seq, wb), lambda b: (b, 0))] * 3
        + [pl.BlockSpec((None, seq * nh, DH), lambda b: (b, layer, 0)) for nh in cache_heads],
        out_shape=[jax.ShapeDtypeStruct((m, wb), BF16)] * 3
        + [jax.ShapeDtypeStruct((n_b, depth * seq * nh, DH), F32) for nh in cache_heads],
        input_output_aliases=aliases,
        compiler_params=_cparams(1),
        name="ctx_attn",
    )(*([proj] * 9), l4, subg, qng, kng, sink, *([] if caches is None else caches))
    return outs[:3], outs[3:]


PREP_TR = 512


def _kv_prep_kernel(ak, av, kv, cosa, sina, cosh, sinh, kng, oak, oav, okv):
    for h in range(HEADS):
        sl = slice(h * DH, (h + 1) * DH)
        oak[:, sl] = _rope(ak[:, sl], cosa[...], sina[...], A_DK // 4).astype(BF16)
    oav[...] = av[...].astype(BF16)
    kvw = KV_HEADS * DH
    for kvh in range(KV_HEADS):
        ksl = slice(kvh * DH, (kvh + 1) * DH)
        bk = _rms(kv[:, ksl]) * kng[...]
        okv[:, ksl] = _rope(bk, cosh[...], sinh[...], DH // 4).astype(BF16)
        dsl = slice(2 * kvw + kvh * DH, 2 * kvw + (kvh + 1) * DH)
        okv[:, dsl] = _rope(kv[:, dsl], cosh[...], sinh[...], DH // 4).astype(BF16)
    okv[:, kvw:2 * kvw] = kv[:, kvw:2 * kvw].astype(BF16)
    okv[:, 3 * kvw:4 * kvw] = kv[:, 3 * kvw:4 * kvw].astype(BF16)


def _kv_prep(proj, n_lat, rope_a, rope_h, kng):
    m = proj.shape[0]
    wb = W_BRANCH
    nt = n_lat // PREP_TR

    def col(c):
        return pl.BlockSpec((PREP_TR, wb), lambda i, c=c: (i, c // wb))

    tab = pl.BlockSpec((PREP_TR, DH), lambda i: (i % nt, 0))
    return pl.pallas_call(
        _kv_prep_kernel,
        grid=(m // PREP_TR,),
        in_specs=[col(COL_AK), col(COL_AV), col(COL_KV), tab, tab, tab, tab,
                  pl.BlockSpec(kng.shape, lambda i: (0, 0))],
        out_specs=[pl.BlockSpec((PREP_TR, wb), lambda i: (i, 0))] * 3,
        out_shape=[jax.ShapeDtypeStruct((m, wb), BF16)] * 3,
        compiler_params=_cparams(1),
        name="kv_prep",
    )(proj, proj, proj, rope_a[0], rope_a[1], rope_h[0], rope_h[1], kng)


LAT_TQ = 256


def _lat_a_kernel(q_ref, g_ref, cos, sin, k_ref, v_ref, kc_ref, vc_ref, l4, subg, o_ref, kall, vall,
                  *, lam_init, n_lat):
    @pl.when(pl.program_id(2) == 0)
    def _():
        kall[0:n_lat, :] = k_ref[...]
        kall[n_lat:, :] = kc_ref[...].astype(BF16)
        vall[0:n_lat, :] = v_ref[...]
        vall[n_lat:, :] = vc_ref[...].astype(BF16)

    lam = _diff_lambda(l4[...], lam_init)
    q = _rope(q_ref[...], cos[...], sin[...], A_DK // 4) * (A_SCALE * LOG2_E)
    q0, q1 = _split_maps(q)
    k = kall[...]

    def probs(qm):
        s = _dot_nt(qm, k)
        p = jnp.exp2(s - s.max(axis=-1, keepdims=True))
        return p, p.sum(axis=-1, keepdims=True)

    p0, l0 = probs(q0)
    p1, l1 = probs(q1)
    w = p0 * (1.0 / l0) - p1 * (lam / l1)
    o = _dot(w.astype(BF16), vall[...])
    o_ref[...] = (_rms(o) * subg[...] * (1.0 - lam_init) * _silu(g_ref[...])).astype(o_ref.dtype)


def _lat_a(proj, akr, avb, cache_k, cache_v, layer, rope_a, l4, subg, lam_init, n_b, n_lat):
    nq = n_lat // LAT_TQ
    n_ctx = cache_k.shape[2]

    def qcol(c):
        return pl.BlockSpec((LAT_TQ, DH), lambda b, h, i, c=c: (b * nq + i, c // DH + h))

    tab = pl.BlockSpec((LAT_TQ, DH), lambda b, h, i: (i, 0))
    kvs = pl.BlockSpec((n_lat, DH), lambda b, h, i: (b, h))
    cs = pl.BlockSpec((None, None, n_ctx, DH), lambda b, h, i: (b, layer, 0, h))

    def full(a):
        return pl.BlockSpec(a.shape, lambda b, h, i: (0,) * a.ndim)

    return pl.pallas_call(
        functools.partial(_lat_a_kernel, lam_init=lam_init, n_lat=n_lat),
        grid=(n_b, HEADS, nq),
        in_specs=[qcol(COL_AQ), qcol(COL_AG), tab, tab, kvs, kvs, cs, cs, full(l4), full(subg)],
        out_specs=pl.BlockSpec((LAT_TQ, DH), lambda b, h, i: (b * nq + i, h)),
        out_shape=jax.ShapeDtypeStruct((n_b * n_lat, W_BRANCH), BF16),
        scratch_shapes=[pltpu.VMEM((n_lat + n_ctx, DH), BF16)] * 2,
        compiler_params=_cparams(3),
        name="lat_a",
    )(proj, proj, rope_a[0], rope_a[1], akr, avb, cache_k, cache_v, l4, subg)


def _lat_b_kernel(q_ref, g_ref, cos, sin, k_ref, v_ref, kc_ref, vc_ref, qng, o_ref, kall, vall, *, n_lat):
    @pl.when(pl.program_id(2) == 0)
    def _():
        kall[0:n_lat, :] = k_ref[...]
        kall[n_lat:, :] = kc_ref[...].astype(BF16)
        vall[0:n_lat, :] = v_ref[...]
        vall[n_lat:, :] = vc_ref[...].astype(BF16)

    k = kall[...]
    v = vall[...]
    for g in range(GQA):
        sl = slice(g * DH, (g + 1) * DH)
        q = _rope(_rms(q_ref[:, sl]) * qng[...], cos[...], sin[...], DH // 4) * H_SCALE
        o = _softmax_pv([_dot_nt(q.astype(BF16), k)], [v])
        o_ref[:, sl] = (o * _silu(g_ref[:, sl])).astype(o_ref.dtype)


def _lat_b(proj, kvb, cache_k, cache_v, layer, rope_h, qng, n_b, n_lat):
    nq = n_lat // LAT_TQ
    n_ctx = cache_k.shape[2]
    gw = GQA * DH

    def qcol(c):
        return pl.BlockSpec((LAT_TQ, gw), lambda b, h, i, c=c: (b * nq + i, c // gw + h))

    tab = pl.BlockSpec((LAT_TQ, DH), lambda b, h, i: (i, 0))
    ks = pl.BlockSpec((n_lat, DH), lambda b, h, i: (b, h))
    vs = pl.BlockSpec((n_lat, DH), lambda b, h, i: (b, KV_HEADS + h))
    cs = pl.BlockSpec((None, None, n_ctx, DH), lambda b, h, i: (b, layer, 0, h))
    return pl.pallas_call(
        functools.partial(_lat_b_kernel, n_lat=n_lat),
        grid=(n_b, KV_HEADS, nq),
        in_specs=[qcol(COL_BQ), qcol(COL_BG), tab, tab, ks, vs, cs, cs,
                  pl.BlockSpec(qng.shape, lambda b, h, i: (0, 0))],
        out_specs=pl.BlockSpec((LAT_TQ, gw), lambda b, h, i: (b * nq + i, h)),
        out_shape=jax.ShapeDtypeStruct((n_b * n_lat, W_BRANCH), BF16),
        scratch_shapes=[pltpu.VMEM((n_lat + n_ctx, DH), BF16)] * 2,
        compiler_params=_cparams(3),
        name="lat_b",
    )(proj, proj, rope_h[0], rope_h[1], kvb, kvb, cache_k, cache_v, qng)


WIN_TQ = 512


def _lat_d_kernel(q_ref, g_ref, cos, sin, k_ref, v_ref, kc_ref, vc_ref, sink, o_ref, kpad, vpad, *, n_lat):
    qi = pl.program_id(2)
    kvh = pl.program_id(1)

    @pl.when(qi == 0)
    def _():
        zeros = jnp.zeros((WINDOW, DH), BF16)
        kpad[0:WINDOW, :] = zeros
        kpad[WINDOW:WINDOW + n_lat, :] = k_ref[...]
        kpad[WINDOW + n_lat:, :] = zeros
        vpad[0:WINDOW, :] = zeros
        vpad[WINDOW:WINDOW + n_lat, :] = v_ref[...]
        vpad[WINDOW + n_lat:, :] = zeros

    span = WIN_TQ + 2 * WINDOW
    start = pl.multiple_of(qi * WIN_TQ, WIN_TQ)
    kw = kpad[pl.ds(start, span), :]
    vw = vpad[pl.ds(start, span), :]
    kc = kc_ref[...].astype(BF16)
    vc = vc_ref[...].astype(BF16)
    k_pos = start - WINDOW + lax.broadcasted_iota(jnp.int32, (1, span), 1)
    q_pos = start + lax.broadcasted_iota(jnp.int32, (WIN_TQ, 1), 0)
    band = (k_pos >= 0) & (k_pos < n_lat) & (jnp.abs(q_pos - k_pos) <= WINDOW)
    lane = lax.broadcasted_iota(jnp.int32, sink.shape, 1)
    sink_row = sink[...]
    for g in range(GQA):
        sl = slice(g * DH, (g + 1) * DH)
        q = (_rope(q_ref[:, sl], cos[...], sin[...], DH // 4) * H_SCALE).astype(BF16)
        s_w = jnp.where(band, _dot_nt(q, kw), NEG_INF)
        s_c = _dot_nt(q, kc)
        sk = jnp.sum(jnp.where(lane == kvh * GQA + g, sink_row, 0.0), axis=-1, keepdims=True)
        o = _softmax_pv([s_w, s_c], [vw, vc], sink=sk)
        o_ref[:, sl] = (o * _silu(g_ref[:, sl])).astype(o_ref.dtype)


def _lat_d(proj, kvb, cache_k, cache_v, layer, rope_h, sink, n_b, n_lat):
    nq = n_lat // WIN_TQ
    n_ctx = cache_k.shape[2]
    gw = GQA * DH

    def qcol(c):
        return pl.BlockSpec((WIN_TQ, gw), lambda b, h, i, c=c: (b * nq + i, c // gw + h))

    tab = pl.BlockSpec((WIN_TQ, DH), lambda b, h, i: (i, 0))
    ks = pl.BlockSpec((n_lat, DH), lambda b, h, i: (b, 2 * KV_HEADS + h))
    vs = pl.BlockSpec((n_lat, DH), lambda b, h, i: (b, 3 * KV_HEADS + h))
    cs = pl.BlockSpec((None, None, n_ctx, DH), lambda b, h, i: (b, layer, 0, h))
    return pl.pallas_call(
        functools.partial(_lat_d_kernel, n_lat=n_lat),
        grid=(n_b, KV_HEADS, nq),
        in_specs=[qcol(COL_DQ), qcol(COL_DG), tab, tab, ks, vs, cs, cs,
                  pl.BlockSpec(sink.shape, lambda b, h, i: (0, 0))],
        out_specs=pl.BlockSpec((WIN_TQ, gw), lambda b, h, i: (b * nq + i, h)),
        out_shape=jax.ShapeDtypeStruct((n_b * n_lat, W_BRANCH), BF16),
        scratch_shapes=[pltpu.VMEM((n_lat + 2 * WINDOW, DH), BF16)] * 2,
        compiler_params=_cparams(3),
        name="lat_d",
    )(proj, proj, rope_h[0], rope_h[1], kvb, kvb, cache_k, cache_v, sink)


def _s5_disc_kernel(lre, lim, ldt, are, aim, fre, fim):
    lam_re = lre[...]
    lam_im = lim[...]
    dt = jnp.exp(ldt[...])
    mag = jnp.exp(lam_re * dt)
    a_re = mag * jnp.cos(lam_im * dt)
    a_im = mag * jnp.sin(lam_im * dt)
    den = lam_re * lam_re + lam_im * lam_im
    nr = a_re - 1.0
    are[...] = a_re
    aim[...] = a_im
    fre[...] = (nr * lam_re + a_im * lam_im) / den
    fim[...] = (a_im * lam_re - nr * lam_im) / den


def _s5_disc(lam_re, lam_im, log_dt):
    r = lam_re.shape[0]
    spec = pl.BlockSpec((r, C_STATE), lambda: (0, 0))
    return pl.pallas_call(
        _s5_disc_kernel,
        in_specs=[spec, spec, pl.BlockSpec((r, 1), lambda: (0, 0))],
        out_specs=[spec] * 4,
        out_shape=[jax.ShapeDtypeStruct((r, C_STATE), F32)] * 4,
        name="s5_disc",
    )(lam_re, lam_im, log_dt)


S5_MM_ROWS = 512


def _s5_kernel(u_ref, r_ref, cm_ref, a_ref, h0_ref, d_ref, y_ref, fin_ref, s_ref, us_ref, *, steps, segmented):
    sl_re = slice(0, STATE_LANES)
    sl_im = slice(STATE_LANES, 2 * STATE_LANES)
    n_rows = steps * CHAINS
    n_mm = n_rows // S5_MM_ROWS

    for i in range(CHAINS):
        us_ref[pl.ds(i, steps, stride=CHAINS), :] = u_ref[i]

    def in_proj(d):
        def body(c, carry):
            r0 = pl.multiple_of(c * S5_MM_ROWS, S5_MM_ROWS)
            s_ref[pl.ds(r0, S5_MM_ROWS), :] = _dot(us_ref[pl.ds(r0, S5_MM_ROWS), :].astype(BF16), r_ref[d])
            return carry
        lax.fori_loop(0, n_mm, body, 0)

    def out_proj(d):
        def body(c, carry):
            r0 = pl.multiple_of(c * S5_MM_ROWS, S5_MM_ROWS)
            y = _dot(s_ref[pl.ds(r0, S5_MM_ROWS), :].astype(BF16), cm_ref[d])
            if d == 0:
                y_ref[pl.ds(r0, S5_MM_ROWS), :] = y + us_ref[pl.ds(r0, S5_MM_ROWS), :] * d_ref[...]
            else:
                y_ref[pl.ds(r0, S5_MM_ROWS), :] += y
            return carry
        lax.fori_loop(0, n_mm, body, 0)

    def scan(d, h_re, h_im, store):
        a_re = jnp.broadcast_to(a_ref[d, 0], (CHAINS, STATE_LANES))
        a_im = jnp.broadcast_to(a_ref[d, 1], (CHAINS, STATE_LANES))

        def body(t, carry):
            hr, hi = carry
            k = t if d == 0 else steps - 1 - t
            r0 = pl.multiple_of(k * CHAINS, CHAINS)
            nr = a_re * hr - a_im * hi + s_ref[pl.ds(r0, CHAINS), sl_re]
            ni = a_re * hi + a_im * hr + s_ref[pl.ds(r0, CHAINS), sl_im]
            if store:
                s_ref[pl.ds(r0, CHAINS), sl_re] = nr
                s_ref[pl.ds(r0, CHAINS), sl_im] = ni
            return nr, ni

        return lax.fori_loop(0, steps, body, (h_re, h_im), unroll=4)

    def segment_starts(d, e_re, e_im, h0_re, h0_im):
        p_re = a_ref[d, 0]
        p_im = a_ref[d, 1]
        n_sq = int(round(math.log2(steps)))
        assert 2 ** n_sq == steps
        for _ in range(n_sq):
            p_re, p_im = p_re * p_re - p_im * p_im, 2.0 * p_re * p_im
        order = list(range(CHAINS)) if d == 0 else list(range(CHAINS - 1, -1, -1))
        first = order[0]
        cur_re = h0_re[first:first + 1]
        cur_im = h0_im[first:first + 1]
        rows_re = {first: cur_re}
        rows_im = {first: cur_im}
        for prev, nxt in zip(order[:-1], order[1:]):
            nre = e_re[prev:prev + 1] + p_re * cur_re - p_im * cur_im
            nim = e_im[prev:prev + 1] + p_re * cur_im + p_im * cur_re
            cur_re, cur_im = nre, nim
            rows_re[nxt] = cur_re
            rows_im[nxt] = cur_im
        row_id = lax.broadcasted_iota(jnp.int32, (CHAINS, STATE_LANES), 0)
        s_re = jnp.zeros((CHAINS, STATE_LANES), F32)
        s_im = jnp.zeros((CHAINS, STATE_LANES), F32)
        for i in range(CHAINS):
            s_re = jnp.where(row_id == i, rows_re[i], s_re)
            s_im = jnp.where(row_id == i, rows_im[i], s_im)
        return s_re, s_im

    for d in range(2):
        in_proj(d)
        h_re = h0_ref[d, 0]
        h_im = h0_ref[d, 1]
        if segmented:
            zero = jnp.zeros((CHAINS, STATE_LANES), F32)
            e_re, e_im = scan(d, zero, zero, store=False)
            h_re, h_im = segment_starts(d, e_re, e_im, h_re, h_im)
        f_re, f_im = scan(d, h_re, h_im, store=True)
        fin_ref[d, 0] = f_re
        fin_ref[d, 1] = f_im
        out_proj(d)


def _s5_scan(proj4, rmat, cmat, a_blk, h0, d_skip, steps, segmented):
    n_g = proj4.shape[0]
    rows = steps * CHAINS
    gl = GROUPS_PER_BLOCK * C_CH
    h0_idx = (lambda g, j: (g, 0, 0, 0, j)) if h0.shape[0] > 1 else (lambda g, j: (0, 0, 0, 0, j))
    return pl.pallas_call(
        functools.partial(_s5_kernel, steps=steps, segmented=segmented),
        grid=(n_g, N_GBLK),
        in_specs=[pl.BlockSpec((None, CHAINS, steps, gl), lambda g, j: (g, 0, 0, COL_CU // gl + j)),
                  pl.BlockSpec((2, None, gl, 2 * STATE_LANES), lambda g, j: (0, j, 0, 0)),
                  pl.BlockSpec((2, None, 2 * STATE_LANES, gl), lambda g, j: (0, j, 0, 0)),
                  pl.BlockSpec((2, 2, 1, STATE_LANES), lambda g, j: (0, 0, 0, j)),
                  pl.BlockSpec((None, 2, 2, CHAINS, STATE_LANES), h0_idx),
                  pl.BlockSpec((1, gl), lambda g, j: (0, j))],
        out_specs=[pl.BlockSpec((None, None, rows, gl), lambda g, j: (g, j, 0, 0)),
                   pl.BlockSpec((None, 2, 2, CHAINS, STATE_LANES), lambda g, j: (g, 0, 0, 0, j))],
        out_shape=[jax.ShapeDtypeStruct((n_g, N_GBLK, rows, gl), F32),
                   jax.ShapeDtypeStruct((n_g, 2, 2, CHAINS, C_GROUPS * C_STATE), F32)],
        scratch_shapes=[pltpu.VMEM((rows, 2 * STATE_LANES), F32), pltpu.VMEM((rows, gl), F32)],
        compiler_params=_cparams(2),
        name="s5_scan",
    )(proj4, rmat, cmat, a_blk, h0, d_skip)


POST_STEPS = 64


def _s5_post_kernel(y_ref, g_ref, w_ref, b_ref, o_ref):
    kt = o_ref.shape[1]
    per_chain = [jnp.concatenate([y_ref[j, pl.ds(i, kt, stride=CHAINS), :] for j in range(N_GBLK)], axis=1)
                 for i in range(CHAINS)]
    y = jax.nn.gelu(jnp.concatenate(per_chain, axis=0))
    z = _dot(y.astype(BF16), w_ref[...]) + b_ref[...]
    y = y * jax.nn.sigmoid(z)
    for i in range(CHAINS):
        o_ref[i] = (y[i * kt:(i + 1) * kt] * _silu(g_ref[i])).astype(o_ref.dtype)


def _s5_post(y_pre, proj4, glu_w_bf16, glu_b):
    n_g, _, steps, _ = proj4.shape
    wb = W_BRANCH
    gl = GROUPS_PER_BLOCK * C_CH
    out = pl.pallas_call(
        _s5_post_kernel,
        grid=(n_g, steps // POST_STEPS),
        in_specs=[pl.BlockSpec((None, N_GBLK, POST_STEPS * CHAINS, gl), lambda g, t: (g, 0, t, 0)),
                  pl.BlockSpec((None, CHAINS, POST_STEPS, wb), lambda g, t: (g, 0, t, COL_CG // wb)),
                  pl.BlockSpec((wb, wb), lambda g, t: (0, 0)),
                  pl.BlockSpec((1, wb), lambda g, t: (0, 0))],
        out_specs=pl.BlockSpec((None, CHAINS, POST_STEPS, wb), lambda g, t: (g, 0, t, 0)),
        out_shape=jax.ShapeDtypeStruct((n_g, CHAINS, steps, wb), BF16),
        compiler_params=_cparams(2),
        name="s5_post",
    )(y_pre, proj4, glu_w_bf16, glu_b)
    return out.reshape(n_g * CHAINS * steps, wb)


OUT_TM = 512
OUT_TN = 512


def _wout_kernel(a_ref, b_ref, c_ref, d_ref, w_ref, x_ref, gate_ref, lg_ref, lb_ref, o_ref):
    n = pl.program_id(1)
    wb = W_BRANCH
    y = _dot(a_ref[...], w_ref[0:wb, :])
    for i, ref in enumerate((b_ref, c_ref, d_ref), start=1):
        y += _dot(ref[...], w_ref[i * wb:(i + 1) * wb, :])
    o_ref[:, pl.ds(pl.multiple_of(n * OUT_TN, OUT_TN), OUT_TN)] = y

    @pl.when(n == pl.num_programs(1) - 1)
    def _():
        gate = gate_ref[...]
        lg = lg_ref[...]
        lb = lb_ref[...]

        def body(i, carry):
            r0 = pl.multiple_of(i * LN_ROWS, LN_ROWS)
            z = DEEPNORM_ALPHA * x_ref[pl.ds(r0, LN_ROWS), :] + gate * o_ref[pl.ds(r0, LN_ROWS), :]
            mu = jnp.mean(z, axis=-1, keepdims=True)
            zc = z - mu
            var = jnp.mean(zc * zc, axis=-1, keepdims=True)
            o_ref[pl.ds(r0, LN_ROWS), :] = zc * lax.rsqrt(var + LN_EPS) * lg + lb
            return carry

        lax.fori_loop(0, o_ref.shape[0] // LN_ROWS, body, 0)


def _wout(branches, w_out_bf16, x, ada_l, row_of_tile, ln_g, ln_b):
    m, d = x.shape
    branch_spec = pl.BlockSpec((OUT_TM, W_BRANCH), lambda mi, k: (mi, 0))
    return pl.pallas_call(
        _wout_kernel,
        grid=(m // OUT_TM, d // OUT_TN),
        in_specs=[branch_spec] * 4
        + [pl.BlockSpec((d, OUT_TN), lambda mi, k: (0, k)),
                  pl.BlockSpec((OUT_TM, d), lambda mi, k: (mi, 0)),
                  pl.BlockSpec((None, 1, d), lambda mi, k: (row_of_tile(mi), 0, 2)),
                  pl.BlockSpec((1, d), lambda mi, k: (0, 0)),
                  pl.BlockSpec((1, d), lambda mi, k: (0, 0))],
        out_specs=pl.BlockSpec((OUT_TM, d), lambda mi, k: (mi, 0)),
        out_shape=jax.ShapeDtypeStruct((m, d), F32),
        compiler_params=_cparams(2),
        name="wout",
    )(*branches, w_out_bf16, x, ada_l, ln_g, ln_b)


def _rope_tables(n_lat, d, tile):
    n_rows = n_lat // GRID_W
    rows = jnp.repeat(jnp.arange(n_rows, dtype=F32), GRID_W)
    cols = jnp.tile(jnp.arange(GRID_W, dtype=F32), n_rows)
    n_freq = d // 4
    inv = ROPE_THETA ** (-jnp.arange(n_freq, dtype=F32) / n_freq)
    ar = rows[:, None] * inv
    ac = cols[:, None] * inv
    ang = jnp.concatenate([ar, ar, ac, ac], axis=-1)
    sign = jnp.concatenate([-jnp.ones((n_freq,), F32), jnp.ones((n_freq,), F32)] * 2)
    cos = jnp.tile(jnp.cos(ang), (1, tile))
    sin = jnp.tile(jnp.sin(ang) * sign, (1, tile))
    return cos, sin


def _block_diag(per_group):
    eye = jnp.eye(GROUPS_PER_BLOCK, dtype=per_group.dtype)
    nb, g, r, c = per_group.shape
    return jnp.einsum('jgrc,gh->jgrhc', per_group, eye).reshape(nb, g * r, g * c)


def kernel(x_prompt, x_sample, c, cache_a_k, cache_a_v, cache_b_k, cache_b_v, cache_d_k, cache_d_v,
           state_c_re, state_c_im, c_ctx, w_ada, b_ada, w_in, w_out, ln_g, ln_b,
           a_lam_q1, a_lam_k1, a_lam_q2, a_lam_k2, a_subln_g, b_qnorm_g, b_knorm_g, d_sink,
           c_lam_re, c_lam_im, c_log_dt, c_b_re, c_b_im, c_c_re, c_c_im, c_d, c_glu_w, c_glu_b):
    n_b_ctx, seq, d_model = x_prompt.shape
    n_b, n_lat, _ = x_sample.shape
    n_ctx = cache_a_k.shape[2]
    depth = w_in.shape[0]
    lam_inits = [0.8 - 0.6 * math.exp(-0.3 * l) for l in range(depth)]

    cond = jnp.concatenate([c_ctx[None], c], axis=0)
    cond_lanes = jnp.broadcast_to(cond[:, :, None], (N_COND, d_model, 128))
    ada = _ada(cond_lanes, w_ada, b_ada).reshape(depth, 8, 1, 3 * d_model)

    bounds = np.concatenate([[0], np.cumsum(IN_SIZES)])
    w_in_r = jnp.concatenate([w_in[:, :, bounds[s]:bounds[s + 1]] for s in _SEG_ORDER], axis=-1).astype(BF16)
    w_out_b = w_out.astype(BF16)
    glu_w_b = c_glu_w.astype(BF16)

    nrow = depth * 2 * C_GROUPS
    a_re, a_im, f_re, f_im = _s5_disc(c_lam_re.reshape(nrow, C_STATE), c_lam_im.reshape(nrow, C_STATE),
                                      c_log_dt.reshape(nrow, 1))
    shp = (depth, 2, C_GROUPS, C_STATE)
    f_re = f_re.reshape(shp)[..., None]
    f_im = f_im.reshape(shp)[..., None]
    bb_re = f_re * c_b_re - f_im * c_b_im
    bb_im = f_re * c_b_im + f_im * c_b_re

    def grp(t):
        return t.reshape(depth * 2 * N_GBLK, GROUPS_PER_BLOCK, t.shape[-2], t.shape[-1])

    r_re = _block_diag(grp(jnp.swapaxes(bb_re, -1, -2)))
    r_im = _block_diag(grp(jnp.swapaxes(bb_im, -1, -2)))
    gl = GROUPS_PER_BLOCK * C_CH
    rmat = jnp.concatenate([r_re, r_im], axis=-1).reshape(depth, 2, N_GBLK, gl, 2 * STATE_LANES).astype(BF16)
    cm_re = _block_diag(grp(jnp.swapaxes(c_c_re, -1, -2)))
    cm_im = _block_diag(grp(jnp.swapaxes(c_c_im, -1, -2)))
    cmat = jnp.concatenate([cm_re, -cm_im], axis=-2).reshape(depth, 2, N_GBLK, 2 * STATE_LANES, gl).astype(BF16)
    a_blk = jnp.stack([a_re.reshape(depth, 2, 1, C_GROUPS * C_STATE),
                       a_im.reshape(depth, 2, 1, C_GROUPS * C_STATE)], axis=2)

    rope_a = _rope_tables(n_lat, A_DK, DH // A_DK)
    rope_h = _rope_tables(n_lat, DH, 1)

    ctx_steps = seq
    n_ctx_groups = n_b_ctx // CHAINS
    lat_steps = n_lat // CHAINS
    h0_ctx = jnp.zeros((1, 2, 2, CHAINS, C_GROUPS * C_STATE), F32)

    ctx_tiles_in = lambda i: 0
    lat_row_in = lambda i: 1 + i // (n_lat // LN_TM)
    lat_row_out = lambda i: 1 + i // (n_lat // OUT_TM)

    xc = x_prompt.reshape(n_b_ctx * seq, d_model)
    xl = x_sample.reshape(n_b * n_lat, d_model)
    new = {k: [] for k in ("sre", "sim")}
    caches = None

    for l in range(depth):
        ada_l = ada[l]
        l4 = jnp.stack([a_lam_q1[l], a_lam_k1[l], a_lam_q2[l], a_lam_k2[l]], axis=0)
        subg = a_subln_g[l][None]
        qng = b_qnorm_g[l][None]
        kng = b_knorm_g[l][None]
        sink = d_sink[l][None]
        d_skip = c_d[l][None]
        glu_b = c_glu_b[l][None]

        proj = _ln_win(xc, ada_l, w_in_r[l], ctx_tiles_in)
        (oa, ob, od), caches = _ctx_attn(proj, seq, l4, subg, qng, kng, sink, lam_inits[l], l, depth, caches)
        proj4 = proj.reshape(n_ctx_groups, CHAINS, ctx_steps, proj.shape[-1])
        y_pre, fin = _s5_scan(proj4, rmat[l], cmat[l], a_blk[l], h0_ctx, d_skip, ctx_steps, segmented=False)
        oc = _s5_post(y_pre, proj4, glu_w_b[l], glu_b)
        xc_new = _wout((oa, ob, oc, od), w_out_b[l], xc, ada_l, ctx_tiles_in, ln_g[l][None], ln_b[l][None])

        fin_b = fin.transpose(0, 3, 1, 2, 4).reshape(n_b_ctx, 2, 2, C_GROUPS, C_STATE)
        new["sre"].append(fin_b[:, :, 0])
        new["sim"].append(fin_b[:, :, 1])

        proj = _ln_win(xl, ada_l, w_in_r[l], lat_row_in)
        akr, avb, kvb = _kv_prep(proj, n_lat, rope_a, rope_h, kng)
        ca_k = cache_a_k.reshape(n_b, depth, n_ctx, HEADS * DH)
        ca_v = cache_a_v.reshape(n_b, depth, n_ctx, HEADS * DH)
        cb_k = cache_b_k.reshape(n_b, depth, n_ctx, KV_HEADS * DH)
        cb_v = cache_b_v.reshape(n_b, depth, n_ctx, KV_HEADS * DH)
        cd_k = cache_d_k.reshape(n_b, depth, n_ctx, KV_HEADS * DH)
        cd_v = cache_d_v.reshape(n_b, depth, n_ctx, KV_HEADS * DH)
        oa = _lat_a(proj, akr, avb, ca_k, ca_v, l, rope_a, l4, subg, lam_inits[l], n_b, n_lat)
        ob = _lat_b(proj, kvb, cb_k, cb_v, l, rope_h, qng, n_b, n_lat)
        od = _lat_d(proj, kvb, cd_k, cd_v, l, rope_h, sink, n_b, n_lat)

        proj4 = proj.reshape(n_b, CHAINS, lat_steps, proj.shape[-1])
        s_re = state_c_re[:, l].reshape(n_b, 2, C_GROUPS * C_STATE)
        s_im = state_c_im[:, l].reshape(n_b, 2, C_GROUPS * C_STATE)
        h0 = jnp.zeros((n_b, 2, 2, CHAINS, C_GROUPS * C_STATE), F32)
        h0 = h0.at[:, 0, 0, 0].set(s_re[:, 0]).at[:, 0, 1, 0].set(s_im[:, 0])
        h0 = h0.at[:, 1, 0, CHAINS - 1].set(s_re[:, 1]).at[:, 1, 1, CHAINS - 1].set(s_im[:, 1])
        y_pre, _ = _s5_scan(proj4, rmat[l], cmat[l], a_blk[l], h0, d_skip, lat_steps, segmented=True)
        oc = _s5_post(y_pre, proj4, glu_w_b[l], glu_b)
        xl_new = _wout((oa, ob, oc, od), w_out_b[l], xl, ada_l, lat_row_out, ln_g[l][None], ln_b[l][None])

        xc, xl = xc_new, xl_new

    stk = lambda k: jnp.stack(new[k], axis=1)
    cache_heads = (HEADS, HEADS, KV_HEADS, KV_HEADS, KV_HEADS, KV_HEADS)
    cache_out = tuple(t.reshape(n_b_ctx, depth, seq, nh, DH) for t, nh in zip(caches, cache_heads))
    return (xc.reshape(n_b_ctx, seq, d_model), xl.reshape(n_b, n_lat, d_model)) + cache_out + (stk("sre"), stk("sim"))
```

```python
import functools
import math

import jax
import jax.numpy as jnp
import numpy as np
from jax import lax
from jax.experimental import pallas as pl
from jax.experimental.pallas import tpu as pltpu

F32 = jnp.float32
BF16 = jnp.bfloat16

D_MODEL = 4096
DEPTH = 2
GRID_W = 64
ROPE_THETA = 10000.0
LN_EPS = 1e-5
RMS_EPS = 1e-6
NEG_INF = -1e30
WINDOW = 128

W_BRANCH = D_MODEL // 4
HEADS = 8
KV_HEADS = 2
GQA = HEADS // KV_HEADS
DH = W_BRANCH // HEADS
A_DK = DH // 2
C_CH = 16
C_GROUPS = W_BRANCH // C_CH
C_STATE = 64
GROUPS_PER_BLOCK = 8
N_GBLK = C_GROUPS // GROUPS_PER_BLOCK
STATE_LANES = GROUPS_PER_BLOCK * C_STATE
CHAINS = 8

IN_SIZES = (W_BRANCH, W_BRANCH, W_BRANCH, W_BRANCH,
            W_BRANCH, KV_HEADS * DH, KV_HEADS * DH, W_BRANCH,
            W_BRANCH, W_BRANCH,
            W_BRANCH, KV_HEADS * DH, KV_HEADS * DH, W_BRANCH)
D_IN = sum(IN_SIZES)
_SEG_ORDER = (0, 1, 2, 3, 4, 7, 8, 9, 10, 13, 5, 6, 11, 12)
COL_AQ, COL_AK, COL_AV, COL_AG = 0, 1024, 2048, 3072
COL_BQ, COL_BG = 4096, 5120
COL_CU, COL_CG = 6144, 7168
COL_DQ, COL_DG = 8192, 9216
COL_KV = 10240
DEEPNORM_ALPHA = (2 * DEPTH) ** 0.25

A_SCALE = A_DK ** -0.5
H_SCALE = DH ** -0.5
LOG2_E = math.log2(math.e)

VMEM_LIMIT = 56 * 1024 * 1024


def _cparams(n_axes):
    return pltpu.CompilerParams(dimension_semantics=("arbitrary",) * n_axes,
                                vmem_limit_bytes=VMEM_LIMIT)


def _dot(a, b):
    return jnp.dot(a, b, preferred_element_type=F32)


def _dot_nt(a, b):
    return lax.dot_general(a, b, (((1,), (1,)), ((), ())), preferred_element_type=F32)


def _silu(x):
    return x * jax.nn.sigmoid(x)


def _rms(x):
    return x * lax.rsqrt(jnp.mean(x * x, axis=-1, keepdims=True) + RMS_EPS)


def _rope(x, cos, sin_signed, quarter):
    lane = lax.broadcasted_iota(jnp.int32, (1, x.shape[-1]), 1)
    first = (lane % (2 * quarter)) < quarter
    rot = jnp.where(first, pltpu.roll(x, x.shape[-1] - quarter, 1), pltpu.roll(x, quarter, 1))
    return x * cos + rot * sin_signed


N_COND = 3
ADA_TN = 512


def _ada_kernel(c_ref, w_ref, b_ref, o_ref, sc_ref):
    @pl.when((pl.program_id(0) == 0) & (pl.program_id(1) == 0))
    def _():
        sc_ref[...] = _silu(c_ref[...])

    tn = w_ref.shape[1]

    def body(i, accs):
        r0 = pl.multiple_of(i * 8, 8)
        w = w_ref[pl.ds(r0, 8), :]
        out = []
        for r, acc in enumerate(accs):
            cs = sc_ref[r, pl.ds(r0, 8), :]
            out.append(acc + w * jnp.concatenate([cs] * (tn // cs.shape[1]), axis=1))
        return tuple(out)

    accs = lax.fori_loop(0, w_ref.shape[0] // 8, body,
                         tuple(jnp.zeros((8, tn), F32) for _ in range(N_COND)), unroll=4)
    row_id = lax.broadcasted_iota(jnp.int32, (8, tn), 0)
    out = jnp.zeros((8, tn), F32)
    for r, a in enumerate(accs):
        out = jnp.where(row_id == r, jnp.sum(a, axis=0, keepdims=True), out)
    o_ref[...] = out + b_ref[...]


def _ada(cond_lanes, w_ada, b_ada):
    depth, d, n = w_ada.shape
    return pl.pallas_call(
        _ada_kernel,
        grid=(depth, n // ADA_TN),
        in_specs=[pl.BlockSpec(cond_lanes.shape, lambda l, j: (0, 0, 0)),
                  pl.BlockSpec((None, d, ADA_TN), lambda l, j: (l, 0, j)),
                  pl.BlockSpec((None, 1, ADA_TN), lambda l, j: (l, 0, j))],
        out_specs=pl.BlockSpec((None, 8, ADA_TN), lambda l, j: (l, 0, j)),
        out_shape=jax.ShapeDtypeStruct((depth, 8, n), F32),
        scratch_shapes=[pltpu.VMEM(cond_lanes.shape, F32)],
        compiler_params=_cparams(2),
        name="ada",
    )(cond_lanes, w_ada, b_ada.reshape(depth, 1, n))


LN_TM = 512
LN_ROWS = 32
WIN_TM = 1024
WIN_TN = 512


def _ln_rows(z, gain, bias):
    mu = jnp.mean(z, axis=-1, keepdims=True)
    zc = z - mu
    var = jnp.mean(zc * zc, axis=-1, keepdims=True)
    return zc * lax.rsqrt(var + LN_EPS) * gain + bias


def _ln_mod_kernel(x_ref, sh_ref, sc_ref, h_ref):
    shift = sh_ref[...]
    scale1 = 1.0 + sc_ref[...]

    def body(i, carry):
        r0 = pl.multiple_of(i * LN_ROWS, LN_ROWS)
        h_ref[pl.ds(r0, LN_ROWS), :] = _ln_rows(x_ref[pl.ds(r0, LN_ROWS), :], scale1, shift).astype(BF16)
        return carry

    lax.fori_loop(0, x_ref.shape[0] // LN_ROWS, body, 0)


def _ln_mod(x, ada_l, row_of_tile):
    m, d = x.shape
    return pl.pallas_call(
        _ln_mod_kernel,
        grid=(m // LN_TM,),
        in_specs=[pl.BlockSpec((LN_TM, d), lambda i: (i, 0)),
                  pl.BlockSpec((None, 1, d), lambda i: (row_of_tile(i), 0, 0)),
                  pl.BlockSpec((None, 1, d), lambda i: (row_of_tile(i), 0, 1))],
        out_specs=pl.BlockSpec((LN_TM, d), lambda i: (i, 0)),
        out_shape=jax.ShapeDtypeStruct((m, d), BF16),
        compiler_params=_cparams(1),
        name="ln_mod",
    )(x, ada_l, ada_l)


def _win_src_blocks():
    bounds = np.concatenate([[0], np.cumsum(IN_SIZES)])
    src = []
    for s in _SEG_ORDER:
        assert bounds[s] % 256 == 0
        src.extend(range(int(bounds[s]) // 256, int(bounds[s + 1]) // 256))
    per = WIN_TN // 256
    blocks = []
    for i in range(0, len(src), per):
        run = src[i:i + per]
        assert run[0] % per == 0 and run == list(range(run[0], run[0] + per))
        blocks.append(run[0] // per)
    return np.asarray(blocks, np.int32)


def _win_kernel(tbl_ref, h_ref, w_ref, o_ref):
    del tbl_ref
    o_ref[...] = _dot(h_ref[...], w_ref[...].astype(BF16)).astype(o_ref.dtype)


def _win(h, w_in, layer):
    m, d = h.shape
    n = w_in.shape[-1]
    tbl = jnp.asarray(_win_src_blocks())
    return pl.pallas_call(
        _win_kernel,
        grid_spec=pltpu.PrefetchScalarGridSpec(
            num_scalar_prefetch=1,
            grid=(m // WIN_TM, n // WIN_TN),
            in_specs=[pl.BlockSpec((WIN_TM, d), lambda i, j, tbl: (i, 0)),
                      pl.BlockSpec((None, d, WIN_TN), lambda i, j, tbl: (layer, 0, tbl[j]))],
            out_specs=pl.BlockSpec((WIN_TM, WIN_TN), lambda i, j, tbl: (i, j)),
        ),
        out_shape=jax.ShapeDtypeStruct((m, n), F32),
        compiler_params=_cparams(2),
        name="win_mm",
    )(tbl, h, w_in)


def _diff_lambda(l4, lam_init):
    e1 = jnp.exp(jnp.sum(l4[0:1] * l4[1:2], axis=-1, keepdims=True))
    e2 = jnp.exp(jnp.sum(l4[2:3] * l4[3:4], axis=-1, keepdims=True))
    return e1 - e2 + lam_init


def _split_maps(q):
    lane = lax.broadcasted_iota(jnp.int32, (1, DH), 1)
    first = lane < A_DK
    return jnp.where(first, q, 0.0).astype(BF16), jnp.where(first, 0.0, q).astype(BF16)


def _softmax_pv(s_list, v_list, sink=None):
    m = s_list[0].max(axis=-1, keepdims=True)
    for s in s_list[1:]:
        m = jnp.maximum(m, s.max(axis=-1, keepdims=True))
    if sink is not None:
        m = jnp.maximum(m, sink)
    den = None
    o = None
    for s, v in zip(s_list, v_list):
        p = jnp.exp(s - m)
        ps = p.sum(axis=-1, keepdims=True)
        den = ps if den is None else den + ps
        pv = _dot(p.astype(BF16), v)
        o = pv if o is None else o + pv
    if sink is not None:
        den = den + jnp.exp(sink - m)
    return o / den


def _diff_finish(o0, o1, lam, lam_init, subg, gate):
    o = o0 - lam * o1
    return _rms(o) * subg * (1.0 - lam_init) * _silu(gate)


def _ctx_attn_kernel(aq, ak, av, ag, bq, bg, dq, dg, kv, l4, subg, qng, kng, sink, *rest, lam_init, layer):
    oa, ob, od = rest[-9:-6]
    seq = aq.shape[0]
    caches = []
    for ref, nh in zip(rest[-6:], (HEADS, HEADS, KV_HEADS, KV_HEADS, KV_HEADS, KV_HEADS)):
        rows = seq * nh
        if ref.shape[0] != rows:
            assert layer == 0
            ref[rows:, :] = jnp.zeros((ref.shape[0] - rows, DH), F32)
        caches.append(ref)
    c_ak, c_av, c_bk, c_bv, c_dk, c_dv = caches
    lam = _diff_lambda(l4[...], lam_init)
    for h in range(HEADS):
        sl = slice(h * DH, (h + 1) * DH)
        q0, q1 = _split_maps(aq[:, sl] * A_SCALE)
        k32 = ak[:, sl]
        v32 = av[:, sl]
        c_ak[pl.ds(h, seq, stride=HEADS), :] = k32
        c_av[pl.ds(h, seq, stride=HEADS), :] = v32
        k = k32.astype(BF16)
        v = v32.astype(BF16)
        o0 = _softmax_pv([_dot_nt(q0, k)], [v])
        o1 = _softmax_pv([_dot_nt(q1, k)], [v])
        oa[:, sl] = _diff_finish(o0, o1, lam, lam_init, subg[...], ag[:, sl]).astype(oa.dtype)

    kvw = KV_HEADS * DH
    for kvh in range(KV_HEADS):
        ksl = slice(kvh * DH, (kvh + 1) * DH)
        bk = _rms(kv[:, ksl]) * kng[...]
        bv = kv[:, kvw + kvh * DH: kvw + (kvh + 1) * DH]
        dk = kv[:, 2 * kvw + kvh * DH: 2 * kvw + (kvh + 1) * DH]
        dv = kv[:, 3 * kvw + kvh * DH: 3 * kvw + (kvh + 1) * DH]
        for c_ref, val in ((c_bk, bk), (c_bv, bv), (c_dk, dk), (c_dv, dv)):
            c_ref[pl.ds(kvh, seq, stride=KV_HEADS), :] = val
        bkb = bk.astype(BF16)
        bvb = bv.astype(BF16)
        dkb = dk.astype(BF16)
        dvb = dv.astype(BF16)
        for g in range(GQA):
            h = kvh * GQA + g
            sl = slice(h * DH, (h + 1) * DH)
            q = (_rms(bq[:, sl]) * qng[...] * H_SCALE).astype(BF16)
            o = _softmax_pv([_dot_nt(q, bkb)], [bvb])
            ob[:, sl] = (o * _silu(bg[:, sl])).astype(ob.dtype)
            q = (dq[:, sl] * H_SCALE).astype(BF16)
            o = _softmax_pv([_dot_nt(q, dkb)], [dvb], sink=sink[:, h:h + 1])
            od[:, sl] = (o * _silu(dg[:, sl])).astype(od.dtype)


def _ctx_attn(proj, seq, l4, subg, qng, kng, sink, lam_init, layer, depth, caches):
    m = proj.shape[0]
    n_b = m // seq
    wb = W_BRANCH

    def col(c):
        return pl.BlockSpec((seq, wb), lambda b, c=c: (b, c // wb))

    def full(a):
        return pl.BlockSpec(a.shape, lambda b: (0,) * a.ndim)

    cache_heads = (HEADS, HEADS, KV_HEADS, KV_HEADS, KV_HEADS, KV_HEADS)
    n_in = 14
    alias_specs = [] if caches is None else [pl.BlockSpec(memory_space=pl.ANY)] * 6
    aliases = {} if caches is None else {n_in + i: 3 + i for i in range(6)}
    if caches is None:
        assert layer == 0
        cache_specs = [pl.BlockSpec((None, depth * seq * nh, DH), lambda b: (b, 0, 0)) for nh in cache_heads]
    else:
        cache_specs = [pl.BlockSpec((None, seq * nh, DH), lambda b: (b, layer, 0)) for nh in cache_heads]
    outs = pl.pallas_call(
        functools.partial(_ctx_attn_kernel, lam_init=lam_init, layer=layer),
        grid=(n_b,),
        in_specs=[col(COL_AQ), col(COL_AK), col(COL_AV), col(COL_AG), col(COL_BQ), col(COL_BG),
                  col(COL_DQ), col(COL_DG), col(COL_KV),
                  full(l4), full(subg), full(qng), full(kng), full(sink)] + alias_specs,
        out_specs=[pl.BlockSpec((seq, wb), lambda b: (b, 0))] * 3 + cache_specs,
        out_shape=[jax.ShapeDtypeStruct((m, wb), BF16)] * 3
        + [jax.ShapeDtypeStruct((n_b, depth * seq * nh, DH), F32) for nh in cache_heads],
        input_output_aliases=aliases,
        compiler_params=_cparams(1),
        name="ctx_attn",
    )(*([proj] * 9), l4, subg, qng, kng, sink, *([] if caches is None else caches))
    return outs[:3], outs[3:]


PREP_TR = 512


def _kv_prep_kernel(ak, av, kv, cosa, sina, cosh, sinh, kng, oak, oav, okv):
    for h in range(HEADS):
        sl = slice(h * DH, (h + 1) * DH)
        oak[:, sl] = _rope(ak[:, sl], cosa[...], sina[...], A_DK // 4).astype(BF16)
    oav[...] = av[...].astype(BF16)
    kvw = KV_HEADS * DH
    for kvh in range(KV_HEADS):
        ksl = slice(kvh * DH, (kvh + 1) * DH)
        bk = _rms(kv[:, ksl]) * kng[...]
        okv[:, ksl] = _rope(bk, cosh[...], sinh[...], DH // 4).astype(BF16)
        dsl = slice(2 * kvw + kvh * DH, 2 * kvw + (kvh + 1) * DH)
        okv[:, dsl] = _rope(kv[:, dsl], cosh[...], sinh[...], DH // 4).astype(BF16)
    okv[:, kvw:2 * kvw] = kv[:, kvw:2 * kvw].astype(BF16)
    okv[:, 3 * kvw:4 * kvw] = kv[:, 3 * kvw:4 * kvw].astype(BF16)


def _kv_prep(proj, n_lat, rope_a, rope_h, kng):
    m = proj.shape[0]
    wb = W_BRANCH
    nt = n_lat // PREP_TR

    def col(c):
        return pl.BlockSpec((PREP_TR, wb), lambda i, c=c: (i, c // wb))

    tab = pl.BlockSpec((PREP_TR, DH), lambda i: (i % nt, 0))
    return pl.pallas_call(
        _kv_prep_kernel,
        grid=(m // PREP_TR,),
        in_specs=[col(COL_AK), col(COL_AV), col(COL_KV), tab, tab, tab, tab,
                  pl.BlockSpec(kng.shape, lambda i: (0, 0))],
        out_specs=[pl.BlockSpec((PREP_TR, wb), lambda i: (i, 0))] * 3,
        out_shape=[jax.ShapeDtypeStruct((m, wb), BF16)] * 3,
        compiler_params=_cparams(1),
        name="kv_prep",
    )(proj, proj, proj, rope_a[0], rope_a[1], rope_h[0], rope_h[1], kng)


LAT_TQ = 256


def _lat_a_kernel(q_ref, g_ref, cos, sin, k_ref, v_ref, kc_ref, vc_ref, l4, subg, o_ref, kall, vall,
                  *, lam_init, n_lat):
    @pl.when(pl.program_id(2) == 0)
    def _():
        kall[0:n_lat, :] = k_ref[...]
        kall[n_lat:, :] = kc_ref[...].astype(BF16)
        vall[0:n_lat, :] = v_ref[...]
        vall[n_lat:, :] = vc_ref[...].astype(BF16)

    lam = _diff_lambda(l4[...], lam_init)
    q = _rope(q_ref[...], cos[...], sin[...], A_DK // 4) * (A_SCALE * LOG2_E)
    q0, q1 = _split_maps(q)
    k = kall[...]

    def probs(qm):
        s = _dot_nt(qm, k)
        p = jnp.exp2(s - s.max(axis=-1, keepdims=True))
        return p, p.sum(axis=-1, keepdims=True)

    p0, l0 = probs(q0)
    p1, l1 = probs(q1)
    w = p0 * (1.0 / l0) - p1 * (lam / l1)
    o = _dot(w.astype(BF16), vall[...])
    o_ref[...] = (_rms(o) * subg[...] * (1.0 - lam_init) * _silu(g_ref[...])).astype(o_ref.dtype)


def _lat_a(proj, akr, avb, cache_k, cache_v, layer, rope_a, l4, subg, lam_init, n_b, n_lat):
    nq = n_lat // LAT_TQ
    n_ctx = cache_k.shape[2]

    def qcol(c):
        return pl.BlockSpec((LAT_TQ, DH), lambda b, h, i, c=c: (b * nq + i, c // DH + h))

    tab = pl.BlockSpec((LAT_TQ, DH), lambda b, h, i: (i, 0))
    kvs = pl.BlockSpec((n_lat, DH), lambda b, h, i: (b, h))
    cs = pl.BlockSpec((None, None, n_ctx, DH), lambda b, h, i: (b, layer, 0, h))

    def full(a):
        return pl.BlockSpec(a.shape, lambda b, h, i: (0,) * a.ndim)

    return pl.pallas_call(
        functools.partial(_lat_a_kernel, lam_init=lam_init, n_lat=n_lat),
        grid=(n_b, HEADS, nq),
        in_specs=[qcol(COL_AQ), qcol(COL_AG), tab, tab, kvs, kvs, cs, cs, full(l4), full(subg)],
        out_specs=pl.BlockSpec((LAT_TQ, DH), lambda b, h, i: (b * nq + i, h)),
        out_shape=jax.ShapeDtypeStruct((n_b * n_lat, W_BRANCH), BF16),
        scratch_shapes=[pltpu.VMEM((n_lat + n_ctx, DH), BF16)] * 2,
        compiler_params=_cparams(3),
        name="lat_a",
    )(proj, proj, rope_a[0], rope_a[1], akr, avb, cache_k, cache_v, l4, subg)


def _lat_b_kernel(q_ref, g_ref, cos, sin, k_ref, v_ref, kc_ref, vc_ref, qng, o_ref, kall, vall, *, n_lat):
    @pl.when(pl.program_id(2) == 0)
    def _():
        kall[0:n_lat, :] = k_ref[...]
        kall[n_lat:, :] = kc_ref[...].astype(BF16)
        vall[0:n_lat, :] = v_ref[...]
        vall[n_lat:, :] = vc_ref[...].astype(BF16)

    k = kall[...]
    v = vall[...]
    for g in range(GQA):
        sl = slice(g * DH, (g + 1) * DH)
        q = _rope(_rms(q_ref[:, sl]) * qng[...], cos[...], sin[...], DH // 4) * H_SCALE
        o = _softmax_pv([_dot_nt(q.astype(BF16), k)], [v])
        o_ref[:, sl] = (o * _silu(g_ref[:, sl])).astype(o_ref.dtype)


def _lat_b(proj, kvb, cache_k, cache_v, layer, rope_h, qng, n_b, n_lat):
    nq = n_lat // LAT_TQ
    n_ctx = cache_k.shape[2]
    gw = GQA * DH

    def qcol(c):
        return pl.BlockSpec((LAT_TQ, gw), lambda b, h, i, c=c: (b * nq + i, c // gw + h))

    tab = pl.BlockSpec((LAT_TQ, DH), lambda b, h, i: (i, 0))
    ks = pl.BlockSpec((n_lat, DH), lambda b, h, i: (b, h))
    vs = pl.BlockSpec((n_lat, DH), lambda b, h, i: (b, KV_HEADS + h))
    cs = pl.BlockSpec((None, None, n_ctx, DH), lambda b, h, i: (b, layer, 0, h))
    return pl.pallas_call(
        functools.partial(_lat_b_kernel, n_lat=n_lat),
        grid=(n_b, KV_HEADS, nq),
        in_specs=[qcol(COL_BQ), qcol(COL_BG), tab, tab, ks, vs, cs, cs,
                  pl.BlockSpec(qng.shape, lambda b, h, i: (0, 0))],
        out_specs=pl.BlockSpec((LAT_TQ, gw), lambda b, h, i: (b * nq + i, h)),
        out_shape=jax.ShapeDtypeStruct((n_b * n_lat, W_BRANCH), BF16),
        scratch_shapes=[pltpu.VMEM((n_lat + n_ctx, DH), BF16)] * 2,
        compiler_params=_cparams(3),
        name="lat_b",
    )(proj, proj, rope_h[0], rope_h[1], kvb, kvb, cache_k, cache_v, qng)


WIN_TQ = 512


def _lat_d_kernel(q_ref, g_ref, cos, sin, k_ref, v_ref, kc_ref, vc_ref, sink, o_ref, kpad, vpad, *, n_lat):
    qi = pl.program_id(2)
    kvh = pl.program_id(1)

    @pl.when(qi == 0)
    def _():
        zeros = jnp.zeros((WINDOW, DH), BF16)
        kpad[0:WINDOW, :] = zeros
        kpad[WINDOW:WINDOW + n_lat, :] = k_ref[...]
        kpad[WINDOW + n_lat:, :] = zeros
        vpad[0:WINDOW, :] = zeros
        vpad[WINDOW:WINDOW + n_lat, :] = v_ref[...]
        vpad[WINDOW + n_lat:, :] = zeros

    span = WIN_TQ + 2 * WINDOW
    start = pl.multiple_of(qi * WIN_TQ, WIN_TQ)
    kw = kpad[pl.ds(start, span), :]
    vw = vpad[pl.ds(start, span), :]
    kc = kc_ref[...].astype(BF16)
    vc = vc_ref[...].astype(BF16)
    k_pos = start - WINDOW + lax.broadcasted_iota(jnp.int32, (1, span), 1)
    q_pos = start + lax.broadcasted_iota(jnp.int32, (WIN_TQ, 1), 0)
    band = (k_pos >= 0) & (k_pos < n_lat) & (jnp.abs(q_pos - k_pos) <= WINDOW)
    lane = lax.broadcasted_iota(jnp.int32, sink.shape, 1)
    sink_row = sink[...]
    for g in range(GQA):
        sl = slice(g * DH, (g + 1) * DH)
        q = (_rope(q_ref[:, sl], cos[...], sin[...], DH // 4) * H_SCALE).astype(BF16)
        s_w = jnp.where(band, _dot_nt(q, kw), NEG_INF)
        s_c = _dot_nt(q, kc)
        sk = jnp.sum(jnp.where(lane == kvh * GQA + g, sink_row, 0.0), axis=-1, keepdims=True)
        o = _softmax_pv([s_w, s_c], [vw, vc], sink=sk)
        o_ref[:, sl] = (o * _silu(g_ref[:, sl])).astype(o_ref.dtype)


def _lat_d(proj, kvb, cache_k, cache_v, layer, rope_h, sink, n_b, n_lat):
    nq = n_lat // WIN_TQ
    n_ctx = cache_k.shape[2]
    gw = GQA * DH

    def qcol(c):
        return pl.BlockSpec((WIN_TQ, gw), lambda b, h, i, c=c: (b * nq + i, c // gw + h))

    tab = pl.BlockSpec((WIN_TQ, DH), lambda b, h, i: (i, 0))
    ks = pl.BlockSpec((n_lat, DH), lambda b, h, i: (b, 2 * KV_HEADS + h))
    vs = pl.BlockSpec((n_lat, DH), lambda b, h, i: (b, 3 * KV_HEADS + h))
    cs = pl.BlockSpec((None, None, n_ctx, DH), lambda b, h, i: (b, layer, 0, h))
    return pl.pallas_call(
        functools.partial(_lat_d_kernel, n_lat=n_lat),
        grid=(n_b, KV_HEADS, nq),
        in_specs=[qcol(COL_DQ), qcol(COL_DG), tab, tab, ks, vs, cs, cs,
                  pl.BlockSpec(sink.shape, lambda b, h, i: (0, 0))],
        out_specs=pl.BlockSpec((WIN_TQ, gw), lambda b, h, i: (b * nq + i, h)),
        out_shape=jax.ShapeDtypeStruct((n_b * n_lat, W_BRANCH), BF16),
        scratch_shapes=[pltpu.VMEM((n_lat + 2 * WINDOW, DH), BF16)] * 2,
        compiler_params=_cparams(3),
        name="lat_d",
    )(proj, proj, rope_h[0], rope_h[1], kvb, kvb, cache_k, cache_v, sink)


def _s5_disc_kernel(lre, lim, ldt, are, aim, fre, fim):
    lam_re = lre[...]
    lam_im = lim[...]
    dt = jnp.exp(ldt[...])
    mag = jnp.exp(lam_re * dt)
    a_re = mag * jnp.cos(lam_im * dt)
    a_im = mag * jnp.sin(lam_im * dt)
    den = lam_re * lam_re + lam_im * lam_im
    nr = a_re - 1.0
    are[...] = a_re
    aim[...] = a_im
    fre[...] = (nr * lam_re + a_im * lam_im) / den
    fim[...] = (a_im * lam_re - nr * lam_im) / den


def _s5_disc(lam_re, lam_im, log_dt):
    r = lam_re.shape[0]
    spec = pl.BlockSpec((r, C_STATE), lambda: (0, 0))
    return pl.pallas_call(
        _s5_disc_kernel,
        in_specs=[spec, spec, pl.BlockSpec((r, 1), lambda: (0, 0))],
        out_specs=[spec] * 4,
        out_shape=[jax.ShapeDtypeStruct((r, C_STATE), F32)] * 4,
        name="s5_disc",
    )(lam_re, lam_im, log_dt)


S5_MM_ROWS = 512


def _s5_kernel(u_ref, r_ref, cm_ref, a_ref, h0_ref, d_ref, y_ref, fin_ref, s_ref, us_ref, *, steps, segmented):
    sl_re = slice(0, STATE_LANES)
    sl_im = slice(STATE_LANES, 2 * STATE_LANES)
    n_rows = steps * CHAINS
    n_mm = n_rows // S5_MM_ROWS

    for i in range(CHAINS):
        us_ref[pl.ds(i, steps, stride=CHAINS), :] = u_ref[i]

    def in_proj(d):
        def body(c, carry):
            r0 = pl.multiple_of(c * S5_MM_ROWS, S5_MM_ROWS)
            s_ref[pl.ds(r0, S5_MM_ROWS), :] = _dot(us_ref[pl.ds(r0, S5_MM_ROWS), :].astype(BF16), r_ref[d])
            return carry
        lax.fori_loop(0, n_mm, body, 0)

    def out_proj(d):
        def body(c, carry):
            r0 = pl.multiple_of(c * S5_MM_ROWS, S5_MM_ROWS)
            y = _dot(s_ref[pl.ds(r0, S5_MM_ROWS), :].astype(BF16), cm_ref[d])
            if d == 0:
                y_ref[pl.ds(r0, S5_MM_ROWS), :] = y + us_ref[pl.ds(r0, S5_MM_ROWS), :] * d_ref[...]
            else:
                y_ref[pl.ds(r0, S5_MM_ROWS), :] += y
            return carry
        lax.fori_loop(0, n_mm, body, 0)

    def scan(d, h_re, h_im, store):
        a_re = jnp.broadcast_to(a_ref[d, 0], (CHAINS, STATE_LANES))
        a_im = jnp.broadcast_to(a_ref[d, 1], (CHAINS, STATE_LANES))

        def body(t, carry):
            hr, hi = carry
            k = t if d == 0 else steps - 1 - t
            r0 = pl.multiple_of(k * CHAINS, CHAINS)
            nr = a_re * hr - a_im * hi + s_ref[pl.ds(r0, CHAINS), sl_re]
            ni = a_re * hi + a_im * hr + s_ref[pl.ds(r0, CHAINS), sl_im]
            if store:
                s_ref[pl.ds(r0, CHAINS), sl_re] = nr
                s_ref[pl.ds(r0, CHAINS), sl_im] = ni
            return nr, ni

        return lax.fori_loop(0, steps, body, (h_re, h_im), unroll=4)

    def segment_starts(d, e_re, e_im, h0_re, h0_im):
        p_re = a_ref[d, 0]
        p_im = a_ref[d, 1]
        n_sq = int(round(math.log2(steps)))
        assert 2 ** n_sq == steps
        for _ in range(n_sq):
            p_re, p_im = p_re * p_re - p_im * p_im, 2.0 * p_re * p_im
        order = list(range(CHAINS)) if d == 0 else list(range(CHAINS - 1, -1, -1))
        first = order[0]
        cur_re = h0_re[first:first + 1]
        cur_im = h0_im[first:first + 1]
        rows_re = {first: cur_re}
        rows_im = {first: cur_im}
        for prev, nxt in zip(order[:-1], order[1:]):
            nre = e_re[prev:prev + 1] + p_re * cur_re - p_im * cur_im
            nim = e_im[prev:prev + 1] + p_re * cur_im + p_im * cur_re
            cur_re, cur_im = nre, nim
            rows_re[nxt] = cur_re
            rows_im[nxt] = cur_im
        row_id = lax.broadcasted_iota(jnp.int32, (CHAINS, STATE_LANES), 0)
        s_re = jnp.zeros((CHAINS, STATE_LANES), F32)
        s_im = jnp.zeros((CHAINS, STATE_LANES), F32)
        for i in range(CHAINS):
            s_re = jnp.where(row_id == i, rows_re[i], s_re)
            s_im = jnp.where(row_id == i, rows_im[i], s_im)
        return s_re, s_im

    for d in range(2):
        in_proj(d)
        h_re = h0_ref[d, 0]
        h_im = h0_ref[d, 1]
        if segmented:
            zero = jnp.zeros((CHAINS, STATE_LANES), F32)
            e_re, e_im = scan(d, zero, zero, store=False)
            h_re, h_im = segment_starts(d, e_re, e_im, h_re, h_im)
        f_re, f_im = scan(d, h_re, h_im, store=True)
        fin_ref[d, 0] = f_re
        fin_ref[d, 1] = f_im
        out_proj(d)


def _s5_scan(proj4, rmat, cmat, a_blk, h0, d_skip, layer, steps, segmented):
    n_g = proj4.shape[0]
    rows = steps * CHAINS
    gl = GROUPS_PER_BLOCK * C_CH
    h0_idx = (lambda g, j: (g, 0, 0, 0, j)) if h0.shape[0] > 1 else (lambda g, j: (0, 0, 0, 0, j))
    return pl.pallas_call(
        functools.partial(_s5_kernel, steps=steps, segmented=segmented),
        grid=(n_g, N_GBLK),
        in_specs=[pl.BlockSpec((None, CHAINS, steps, gl), lambda g, j: (g, 0, 0, COL_CU // gl + j)),
                  pl.BlockSpec((None, 2, None, gl, 2 * STATE_LANES), lambda g, j: (layer, 0, j, 0, 0)),
                  pl.BlockSpec((None, 2, None, 2 * STATE_LANES, gl), lambda g, j: (layer, 0, j, 0, 0)),
                  pl.BlockSpec((None, 2, 2, 1, STATE_LANES), lambda g, j: (layer, 0, 0, 0, j)),
                  pl.BlockSpec((None, 2, 2, CHAINS, STATE_LANES), h0_idx),
                  pl.BlockSpec((None, 1, gl), lambda g, j: (layer, 0, j))],
        out_specs=[pl.BlockSpec((None, None, rows, gl), lambda g, j: (g, j, 0, 0)),
                   pl.BlockSpec((None, 2, 2, CHAINS, STATE_LANES), lambda g, j: (g, 0, 0, 0, j))],
        out_shape=[jax.ShapeDtypeStruct((n_g, N_GBLK, rows, gl), F32),
                   jax.ShapeDtypeStruct((n_g, 2, 2, CHAINS, C_GROUPS * C_STATE), F32)],
        scratch_shapes=[pltpu.VMEM((rows, 2 * STATE_LANES), F32), pltpu.VMEM((rows, gl), F32)],
        compiler_params=_cparams(2),
        name="s5_scan",
    )(proj4, rmat, cmat, a_blk, h0, d_skip)


POST_STEPS = 64


def _s5_post_kernel(y_ref, g_ref, w_ref, b_ref, o_ref):
    kt = o_ref.shape[1]
    per_chain = [jnp.concatenate([y_ref[j, pl.ds(i, kt, stride=CHAINS), :] for j in range(N_GBLK)], axis=1)
                 for i in range(CHAINS)]
    y = jax.nn.gelu(jnp.concatenate(per_chain, axis=0))
    z = _dot(y.astype(BF16), w_ref[...]) + b_ref[...]
    y = y * jax.nn.sigmoid(z)
    for i in range(CHAINS):
        o_ref[i] = (y[i * kt:(i + 1) * kt] * _silu(g_ref[i])).astype(o_ref.dtype)


def _s5_post(y_pre, proj4, glu_w_bf16, glu_b, layer):
    n_g, _, steps, _ = proj4.shape
    wb = W_BRANCH
    gl = GROUPS_PER_BLOCK * C_CH
    out = pl.pallas_call(
        _s5_post_kernel,
        grid=(n_g, steps // POST_STEPS),
        in_specs=[pl.BlockSpec((None, N_GBLK, POST_STEPS * CHAINS, gl), lambda g, t: (g, 0, t, 0)),
                  pl.BlockSpec((None, CHAINS, POST_STEPS, wb), lambda g, t: (g, 0, t, COL_CG // wb)),
                  pl.BlockSpec((None, wb, wb), lambda g, t: (layer, 0, 0)),
                  pl.BlockSpec((None, 1, wb), lambda g, t: (layer, 0, 0))],
        out_specs=pl.BlockSpec((None, CHAINS, POST_STEPS, wb), lambda g, t: (g, 0, t, 0)),
        out_shape=jax.ShapeDtypeStruct((n_g, CHAINS, steps, wb), BF16),
        compiler_params=_cparams(2),
        name="s5_post",
    )(y_pre, proj4, glu_w_bf16, glu_b)
    return out.reshape(n_g * CHAINS * steps, wb)


OUT_TM = 512
OUT_TN = 512


def _wout_kernel(a_ref, b_ref, c_ref, d_ref, w_ref, x_hbm, gate_ref, lg_ref, lb_ref, *rest, emit_h):
    if emit_h:
        nsh_ref, nsc_ref, o_ref, h_ref, x_buf, x_sem = rest
    else:
        o_ref, x_buf, x_sem = rest
    mi = pl.program_id(0)
    n = pl.program_id(1)
    wb = W_BRANCH

    def x_copy():
        return pltpu.make_async_copy(x_hbm.at[pl.ds(pl.multiple_of(mi * OUT_TM, OUT_TM), OUT_TM), :], x_buf, x_sem)

    @pl.when(n == 0)
    def _():
        x_copy().start()

    y = _dot(a_ref[...], w_ref[0:wb, :])
    for i, ref in enumerate((b_ref, c_ref, d_ref), start=1):
        y += _dot(ref[...], w_ref[i * wb:(i + 1) * wb, :])
    o_ref[:, pl.ds(pl.multiple_of(n * OUT_TN, OUT_TN), OUT_TN)] = y

    @pl.when(n == pl.num_programs(1) - 1)
    def _():
        x_copy().wait()
        gate = gate_ref[...]
        lg = lg_ref[...]
        lb = lb_ref[...]
        if emit_h:
            shift = nsh_ref[...]
            scale1 = 1.0 + nsc_ref[...]

        def body(i, carry):
            r0 = pl.multiple_of(i * LN_ROWS, LN_ROWS)
            z = DEEPNORM_ALPHA * x_buf[pl.ds(r0, LN_ROWS), :] + gate * o_ref[pl.ds(r0, LN_ROWS), :]
            xn = _ln_rows(z, lg, lb)
            o_ref[pl.ds(r0, LN_ROWS), :] = xn
            if emit_h:
                h_ref[pl.ds(r0, LN_ROWS), :] = _ln_rows(xn, scale1, shift).astype(BF16)
            return carry

        lax.fori_loop(0, o_ref.shape[0] // LN_ROWS, body, 0)


def _wout(branches, w_out_bf16, layer, x, ada_l, ada_next, row_of_tile, ln_g, ln_b):
    m, d = x.shape
    emit_h = ada_next is not None
    branch_spec = pl.BlockSpec((OUT_TM, W_BRANCH), lambda mi, k: (mi, 0))
    row_spec = lambda col: pl.BlockSpec((None, 1, d), lambda mi, k: (row_of_tile(mi), 0, col))
    vec_spec = pl.BlockSpec((None, 1, d), lambda mi, k: (layer, 0, 0))
    full_rows = pl.BlockSpec((OUT_TM, d), lambda mi, k: (mi, 0))
    outs = pl.pallas_call(
        functools.partial(_wout_kernel, emit_h=emit_h),
        grid=(m // OUT_TM, d // OUT_TN),
        in_specs=[branch_spec] * 4
        + [pl.BlockSpec((None, d, OUT_TN), lambda mi, k: (layer, 0, k)),
           pl.BlockSpec(memory_space=pl.ANY), row_spec(2), vec_spec, vec_spec]
        + ([row_spec(0), row_spec(1)] if emit_h else []),
        out_specs=[full_rows] * (2 if emit_h else 1),
        out_shape=[jax.ShapeDtypeStruct((m, d), F32)] + ([jax.ShapeDtypeStruct((m, d), BF16)] if emit_h else []),
        scratch_shapes=[pltpu.VMEM((OUT_TM, d), F32), pltpu.SemaphoreType.DMA(())],
        compiler_params=_cparams(2),
        name="wout",
    )(*branches, w_out_bf16, x, ada_l, ln_g, ln_b, *([ada_next, ada_next] if emit_h else []))
    return (outs[0], outs[1]) if emit_h else (outs[0], None)


def _rope_tables(n_lat, d, tile):
    n_rows = n_lat // GRID_W
    rows = jnp.repeat(jnp.arange(n_rows, dtype=F32), GRID_W)
    cols = jnp.tile(jnp.arange(GRID_W, dtype=F32), n_rows)
    n_freq = d // 4
    inv = ROPE_THETA ** (-jnp.arange(n_freq, dtype=F32) / n_freq)
    ar = rows[:, None] * inv
    ac = cols[:, None] * inv
    ang = jnp.concatenate([ar, ar, ac, ac], axis=-1)
    sign = jnp.concatenate([-jnp.ones((n_freq,), F32), jnp.ones((n_freq,), F32)] * 2)
    cos = jnp.tile(jnp.cos(ang), (1, tile))
    sin = jnp.tile(jnp.sin(ang) * sign, (1, tile))
    return cos, sin


def _block_diag(per_group):
    eye = jnp.eye(GROUPS_PER_BLOCK, dtype=per_group.dtype)
    nb, g, r, c = per_group.shape
    return jnp.einsum('jgrc,gh->jgrhc', per_group, eye).reshape(nb, g * r, g * c)


def kernel(x_prompt, x_sample, c, cache_a_k, cache_a_v, cache_b_k, cache_b_v, cache_d_k, cache_d_v,
           state_c_re, state_c_im, c_ctx, w_ada, b_ada, w_in, w_out, ln_g, ln_b,
           a_lam_q1, a_lam_k1, a_lam_q2, a_lam_k2, a_subln_g, b_qnorm_g, b_knorm_g, d_sink,
           c_lam_re, c_lam_im, c_log_dt, c_b_re, c_b_im, c_c_re, c_c_im, c_d, c_glu_w, c_glu_b):
    n_b_ctx, seq, d_model = x_prompt.shape
    n_b, n_lat, _ = x_sample.shape
    n_ctx = cache_a_k.shape[2]
    depth = w_in.shape[0]
    lam_inits = [0.8 - 0.6 * math.exp(-0.3 * l) for l in range(depth)]

    cond = jnp.concatenate([c_ctx[None], c], axis=0)
    cond_lanes = jnp.broadcast_to(cond[:, :, None], (N_COND, d_model, 128))
    ada = _ada(cond_lanes, w_ada, b_ada).reshape(depth, 8, 1, 3 * d_model)

    w_out_b = w_out.astype(BF16)
    glu_w_b = c_glu_w.astype(BF16)
    ln_g3 = ln_g.reshape(depth, 1, d_model)
    ln_b3 = ln_b.reshape(depth, 1, d_model)
    d_skip = c_d.reshape(depth, 1, W_BRANCH)
    glu_b = c_glu_b.reshape(depth, 1, W_BRANCH)

    nrow = depth * 2 * C_GROUPS
    a_re, a_im, f_re, f_im = _s5_disc(c_lam_re.reshape(nrow, C_STATE), c_lam_im.reshape(nrow, C_STATE),
                                      c_log_dt.reshape(nrow, 1))
    shp = (depth, 2, C_GROUPS, C_STATE)
    f_re = f_re.reshape(shp)[..., None]
    f_im = f_im.reshape(shp)[..., None]
    bb_re = f_re * c_b_re - f_im * c_b_im
    bb_im = f_re * c_b_im + f_im * c_b_re

    def grp(t):
        return t.reshape(depth * 2 * N_GBLK, GROUPS_PER_BLOCK, t.shape[-2], t.shape[-1])

    r_re = _block_diag(grp(jnp.swapaxes(bb_re, -1, -2)))
    r_im = _block_diag(grp(jnp.swapaxes(bb_im, -1, -2)))
    gl = GROUPS_PER_BLOCK * C_CH
    rmat = jnp.concatenate([r_re, r_im], axis=-1).reshape(depth, 2, N_GBLK, gl, 2 * STATE_LANES).astype(BF16)
    cm_re = _block_diag(grp(jnp.swapaxes(c_c_re, -1, -2)))
    cm_im = _block_diag(grp(jnp.swapaxes(c_c_im, -1, -2)))
    cmat = jnp.concatenate([cm_re, -cm_im], axis=-2).reshape(depth, 2, N_GBLK, 2 * STATE_LANES, gl).astype(BF16)
    a_blk = jnp.stack([a_re.reshape(depth, 2, 1, C_GROUPS * C_STATE),
                       a_im.reshape(depth, 2, 1, C_GROUPS * C_STATE)], axis=2)

    rope_a = _rope_tables(n_lat, A_DK, DH // A_DK)
    rope_h = _rope_tables(n_lat, DH, 1)

    ctx_steps = seq
    n_ctx_groups = n_b_ctx // CHAINS
    lat_steps = n_lat // CHAINS
    h0_ctx = jnp.zeros((1, 2, 2, CHAINS, C_GROUPS * C_STATE), F32)

    ctx_tiles_in = lambda i: 0
    lat_row_in = lambda i: 1 + i // (n_lat // LN_TM)
    lat_row_out = lambda i: 1 + i // (n_lat // OUT_TM)

    xc = x_prompt.reshape(n_b_ctx * seq, d_model)
    xl = x_sample.reshape(n_b * n_lat, d_model)
    new = {k: [] for k in ("sre", "sim")}
    caches = None
    hc = _ln_mod(xc, ada[0], ctx_tiles_in)
    hl = _ln_mod(xl, ada[0], lat_row_in)

    for l in range(depth):
        ada_l = ada[l]
        ada_next = ada[l + 1] if l + 1 < depth else None
        l4 = jnp.stack([a_lam_q1[l], a_lam_k1[l], a_lam_q2[l], a_lam_k2[l]], axis=0)
        subg = a_subln_g[l][None]
        qng = b_qnorm_g[l][None]
        kng = b_knorm_g[l][None]
        sink = d_sink[l][None]

        proj = _win(hc, w_in, l)
        (oa, ob, od), caches = _ctx_attn(proj, seq, l4, subg, qng, kng, sink, lam_inits[l], l, depth, caches)
        proj4 = proj.reshape(n_ctx_groups, CHAINS, ctx_steps, proj.shape[-1])
        y_pre, fin = _s5_scan(proj4, rmat, cmat, a_blk, h0_ctx, d_skip, l, ctx_steps, segmented=False)
        oc = _s5_post(y_pre, proj4, glu_w_b, glu_b, l)
        xc_new, hc = _wout((oa, ob, oc, od), w_out_b, l, xc, ada_l, ada_next, ctx_tiles_in, ln_g3, ln_b3)

        fin_b = fin.transpose(0, 3, 1, 2, 4).reshape(n_b_ctx, 2, 2, C_GROUPS, C_STATE)
        new["sre"].append(fin_b[:, :, 0])
        new["sim"].append(fin_b[:, :, 1])

        proj = _win(hl, w_in, l)
        akr, avb, kvb = _kv_prep(proj, n_lat, rope_a, rope_h, kng)
        ca_k = cache_a_k.reshape(n_b, depth, n_ctx, HEADS * DH)
        ca_v = cache_a_v.reshape(n_b, depth, n_ctx, HEADS * DH)
        cb_k = cache_b_k.reshape(n_b, depth, n_ctx, KV_HEADS * DH)
        cb_v = cache_b_v.reshape(n_b, depth, n_ctx, KV_HEADS * DH)
        cd_k = cache_d_k.reshape(n_b, depth, n_ctx, KV_HEADS * DH)
        cd_v = cache_d_v.reshape(n_b, depth, n_ctx, KV_HEADS * DH)
        oa = _lat_a(proj, akr, avb, ca_k, ca_v, l, rope_a, l4, subg, lam_inits[l], n_b, n_lat)
        ob = _lat_b(proj, kvb, cb_k, cb_v, l, rope_h, qng, n_b, n_lat)
        od = _lat_d(proj, kvb, cd_k, cd_v, l, rope_h, sink, n_b, n_lat)

        proj4 = proj.reshape(n_b, CHAINS, lat_steps, proj.shape[-1])
        s_re = state_c_re[:, l].reshape(n_b, 2, C_GROUPS * C_STATE)
        s_im = state_c_im[:, l].reshape(n_b, 2, C_GROUPS * C_STATE)
        h0 = jnp.zeros((n_b, 2, 2, CHAINS, C_GROUPS * C_STATE), F32)
        h0 = h0.at[:, 0, 0, 0].set(s_re[:, 0]).at[:, 0, 1, 0].set(s_im[:, 0])
        h0 = h0.at[:, 1, 0, CHAINS - 1].set(s_re[:, 1]).at[:, 1, 1, CHAINS - 1].set(s_im[:, 1])
        y_pre, _ = _s5_scan(proj4, rmat, cmat, a_blk, h0, d_skip, l, lat_steps, segmented=True)
        oc = _s5_post(y_pre, proj4, glu_w_b, glu_b, l)
        xl_new, hl = _wout((oa, ob, oc, od), w_out_b, l, xl, ada_l, ada_next, lat_row_out, ln_g3, ln_b3)

        xc, xl = xc_new, xl_new

    stk = lambda k: jnp.stack(new[k], axis=1)
    cache_heads = (HEADS, HEADS, KV_HEADS, KV_HEADS, KV_HEADS, KV_HEADS)
    cache_out = tuple(t.reshape(n_b_ctx, depth, seq, nh, DH) for t, nh in zip(caches, cache_heads))
    return (xc.reshape(n_b_ctx, seq, d_model), xl.reshape(n_b, n_lat, d_model)) + cache_out + (stk("sre"), stk("sim"))
```
